```python
import math
import jax, jax.numpy as jnp
from jax import lax
import numpy as np

D_MODEL = 2048
BATCH = 2
SEQ = 8192
DEPTH = 4

N_MIXERS = 2
N_RWKV = (DEPTH + 1) // 2
N_DIFF = DEPTH // 2
RWKV_HEAD = 64
RWKV_HEADS = D_MODEL // RWKV_HEAD
LORA_DECAY = 96
LORA_AAA = 96
LORA_MV = 64
LORA_GATE = 128
GN_EPS = 64e-5
DIFF_QK = 128
DIFF_V = 256
DIFF_HEADS = D_MODEL // DIFF_V
Q_BLOCK = 128
REL_BUCKETS = 32
REL_MAX_EXACT = 16
REL_MAX_DIST = 128
D_FF = ((8 * D_MODEL // 3 + 127) // 128) * 128
CONV_W = 3
LN_EPS = 1e-5
ALPHA = (2 * DEPTH) ** 0.25
BETA = (8 * DEPTH) ** -0.25

kernel_name = "hybrid_rwkv7_diffattn_convglu_deepnorm"


def layer_norm(x, g, b):
    xf = x.astype(jnp.float32)
    mu = xf.mean(-1, keepdims=True)
    var = jnp.square(xf - mu).mean(-1, keepdims=True)
    return ((xf - mu) * lax.rsqrt(var + LN_EPS) * g + b).astype(x.dtype)


def token_shift(x):
    return jnp.pad(x, ((0, 0), (1, 0), (0, 0)))[:, :-1]


def wkv7_scan(r, decay, k, v, a_vec, b_vec):
    B, T, H, N = r.shape

    def step(S, inp):
        r_t, w_t, k_t, v_t, a_t, b_t = inp
        sa = jnp.einsum('bhij,bhj->bhi', S, a_t)
        S = S * w_t[:, :, None, :] + sa[..., None] * b_t[:, :, None, :] + v_t[..., None] * k_t[:, :, None, :]
        y = jnp.einsum('bhij,bhj->bhi', S, r_t)
        return S, y

    xs = tuple(jnp.moveaxis(t.astype(jnp.float32), 1, 0) for t in (r, decay, k, v, a_vec, b_vec))
    S0 = jnp.zeros((B, H, N, N), jnp.float32)
    _, y = lax.scan(step, S0, xs)
    return jnp.moveaxis(y, 0, 1)


def rwkv7_mix(x, v_first, vres, mu, w_rkv, w0, w1, w2, a0, a1, a2, g1, g2, k_k, k_a, r_k, lnx_g, lnx_b, w_o):
    B, T, D = x.shape
    H, N = RWKV_HEADS, RWKV_HEAD
    xx = token_shift(x) - x
    xr, xw, xk, xv, xa, xg = (x + xx * mu[i] for i in range(6))
    r = xr @ w_rkv[0]
    k = xk @ w_rkv[1]
    v = xv @ w_rkv[2]
    w = -jax.nn.softplus(-(w0 + jnp.tanh(xw @ w1) @ w2)) - 0.5
    if vres is None:
        v_first = v
    else:
        v0, v1, v2 = vres
        v = v + (v_first - v) * jax.nn.sigmoid(v0 + (xv @ v1) @ v2)
    a = jax.nn.sigmoid(a0 + (xa @ a1) @ a2)
    g = jax.nn.sigmoid(xg @ g1) @ g2
    kk = (k * k_k).reshape(B, T, H, N).astype(jnp.float32)
    kk = kk / jnp.maximum(jnp.sqrt(jnp.sum(kk * kk, -1, keepdims=True)), 1e-12)
    k = k * (1 + (a - 1) * k_a)
    r_h = r.reshape(B, T, H, N)
    k_h = k.reshape(B, T, H, N)
    v_h = v.reshape(B, T, H, N)
    a_h = a.reshape(B, T, H, N).astype(jnp.float32)
    decay = jnp.exp(-jnp.exp(w.astype(jnp.float32))).reshape(B, T, H, N)
    y = wkv7_scan(r_h, decay, k_h, v_h, -kk, kk * a_h)
    mean = y.mean(-1, keepdims=True)
    var = jnp.square(y - mean).mean(-1, keepdims=True)
    y = ((y - mean) * lax.rsqrt(var + GN_EPS)).reshape(B, T, D) * lnx_g + lnx_b
    bonus = jnp.sum(r_h * k_h * r_k, -1, keepdims=True) * v_h
    y = (y + bonus.reshape(B, T, D)).astype(x.dtype)
    return (y * g) @ w_o, v_first


def rel_bucket(dist):
    n = jnp.maximum(dist, 0)
    large = REL_MAX_EXACT + (jnp.log(jnp.maximum(n, 1).astype(jnp.float32) / REL_MAX_EXACT)
                             / math.log(REL_MAX_DIST / REL_MAX_EXACT)
                             * (REL_BUCKETS - REL_MAX_EXACT)).astype(jnp.int32)
    large = jnp.minimum(large, REL_BUCKETS - 1)
    return jnp.where(n < REL_MAX_EXACT, n, large)


def diff_attn_mix(x, w_qkv, lam, subln_g, w_o, rel_bias, lambda_init):
    B, T, D = x.shape
    H = DIFF_HEADS
    q, k, v = jnp.split(x @ w_qkv, 3, axis=-1)
    q = q.reshape(B, T, H, 2, DIFF_QK) * (DIFF_QK ** -0.5)
    k = k.reshape(B, T, H, 2, DIFF_QK)
    v = v.reshape(B, T, H, DIFF_V)
    lamf = lam.astype(jnp.float32)
    lam_full = jnp.exp(jnp.sum(lamf[0] * lamf[1])) - jnp.exp(jnp.sum(lamf[2] * lamf[3])) + lambda_init
    k_pos = jnp.arange(T)
    n_blk = T // Q_BLOCK

    def block(i):
        start = i * Q_BLOCK
        q_blk = lax.dynamic_slice_in_dim(q, start, Q_BLOCK, axis=1)
        s = jnp.einsum('bqhcd,bkhcd->bhcqk', q_blk, k).astype(jnp.float32)
        dist = (start + jnp.arange(Q_BLOCK))[:, None] - k_pos[None, :]
        bias = jnp.transpose(rel_bias[rel_bucket(dist)], (2, 0, 1))
        s = s + bias[None, :, None].astype(jnp.float32)
        s = jnp.where(dist >= 0, s, -1e30)
        p = jax.nn.softmax(s, axis=-1)
        attn = p[:, :, 0] - lam_full * p[:, :, 1]
        return jnp.einsum('bhqk,bkhd->bqhd', attn.astype(v.dtype), v)

    o = lax.map(block, jnp.arange(n_blk))
    o = jnp.moveaxis(o, 0, 1).reshape(B, T, H, DIFF_V).astype(jnp.float32)
    o = o * lax.rsqrt(jnp.mean(o * o, -1, keepdims=True) + LN_EPS) * subln_g * (1.0 - lambda_init)
    return o.reshape(B, T, D).astype(x.dtype) @ w_o


def conv_glu(x, w_up, conv_w, conv_b, w_down):
    T = x.shape[1]
    u, g = jnp.split(x @ w_up, 2, axis=-1)
    gp = jnp.pad(g, ((0, 0), (CONV_W - 1, 0), (0, 0)))
    gc = conv_b
    for j in range(CONV_W):
        gc = gc + gp[:, j:j + T] * conv_w[j]
    return (u * jax.nn.gelu(gc)) @ w_down


def setup_inputs(seed: int = 0) -> dict:
    key = jax.random.key(seed)
    ks = iter(jax.random.split(key, 40))
    nrm = lambda shape, s: jax.random.normal(next(ks), shape, jnp.float32) * s
    D, H, N, F = D_MODEL, RWKV_HEADS, RWKV_HEAD, D_FF
    nv = max(N_RWKV - 1, 1)
    ramp = (jnp.arange(D, dtype=jnp.float32) / (D - 1)) ** 0.85
    return {
        "x": nrm((BATCH, SEQ, D), 1.0),
        "ln_g": 1.0 + nrm((DEPTH, 2, D), 0.05),
        "ln_b": nrm((DEPTH, 2, D), 0.02),
        "rw_mu": jax.random.uniform(next(ks), (N_RWKV, 6, D), jnp.float32),
        "rw_w_rkv": nrm((N_RWKV, 3, D, D), D ** -0.5),
        "rw_w0": -6.0 + 5.0 * ramp[None] + nrm((N_RWKV, D), 0.1),
        "rw_w1": nrm((N_RWKV, D, LORA_DECAY), D ** -0.5),
        "rw_w2": nrm((N_RWKV, LORA_DECAY, D), 0.1 * LORA_DECAY ** -0.5),
        "rw_a0": nrm((N_RWKV, D), 0.1),
        "rw_a1": nrm((N_RWKV, D, LORA_AAA), D ** -0.5),
        "rw_a2": nrm((N_RWKV, LORA_AAA, D), 0.3 * LORA_AAA ** -0.5),
        "rw_v0": nrm((nv, D), 0.1),
        "rw_v1": nrm((nv, D, LORA_MV), D ** -0.5),
        "rw_v2": nrm((nv, LORA_MV, D), 0.3 * LORA_MV ** -0.5),
        "rw_g1": nrm((N_RWKV, D, LORA_GATE), D ** -0.5),
        "rw_g2": nrm((N_RWKV, LORA_GATE, D), LORA_GATE ** -0.5),
        "rw_k_k": 0.85 + nrm((N_RWKV, D), 0.05),
        "rw_k_a": 1.0 + nrm((N_RWKV, D), 0.05),
        "rw_r_k": nrm((N_RWKV, H, N), 0.1),
        "rw_lnx_g": 1.0 + nrm((N_RWKV, D), 0.05),
        "rw_lnx_b": nrm((N_RWKV, D), 0.02),
        "rw_w_o": nrm((N_RWKV, D, D), BETA * D ** -0.5),
        "da_w_qkv": nrm((N_DIFF, D, 3 * D), D ** -0.5),
        "da_lam": nrm((N_DIFF, 4, DIFF_QK), 0.1),
        "da_subln_g": 1.0 + nrm((N_DIFF, DIFF_V), 0.05),
        "da_w_o": nrm((N_DIFF, D, D), BETA * D ** -0.5),
        "rel_bias": nrm((REL_BUCKETS, DIFF_HEADS), 0.5),
        "ff_w_up": nrm((DEPTH, D, 2 * F), D ** -0.5),
        "ff_conv_w": nrm((DEPTH, CONV_W, F), CONV_W ** -0.5),
        "ff_conv_b": nrm((DEPTH, F), 0.02),
        "ff_w_down": nrm((DEPTH, F, D), BETA * F ** -0.5),
    }


def reference(x, ln_g, ln_b, rw_mu, rw_w_rkv, rw_w0, rw_w1, rw_w2, rw_a0, rw_a1, rw_a2,
              rw_v0, rw_v1, rw_v2, rw_g1, rw_g2, rw_k_k, rw_k_a, rw_r_k, rw_lnx_g, rw_lnx_b, rw_w_o,
              da_w_qkv, da_lam, da_subln_g, da_w_o, rel_bias,
              ff_w_up, ff_conv_w, ff_conv_b, ff_w_down):
    v_first = None
    for i in range(DEPTH):
        j = i // N_MIXERS
        if i % N_MIXERS == 0:
            vres = None if j == 0 else (rw_v0[j - 1], rw_v1[j - 1], rw_v2[j - 1])
            h, v_first = rwkv7_mix(x, v_first, vres, rw_mu[j], rw_w_rkv[j], rw_w0[j], rw_w1[j], rw_w2[j],
                                   rw_a0[j], rw_a1[j], rw_a2[j], rw_g1[j], rw_g2[j], rw_k_k[j], rw_k_a[j],
                                   rw_r_k[j], rw_lnx_g[j], rw_lnx_b[j], rw_w_o[j])
        else:
            lambda_init = 0.8 - 0.6 * math.exp(-0.3 * i)
            h = diff_attn_mix(x, da_w_qkv[j], da_lam[j], da_subln_g[j], da_w_o[j], rel_bias, lambda_init)
        x = layer_norm(ALPHA * x + h, ln_g[i, 0], ln_b[i, 0])
        x = layer_norm(ALPHA * x + conv_glu(x, ff_w_up[i], ff_conv_w[i], ff_conv_b[i], ff_w_down[i]),
                       ln_g[i, 1], ln_b[i, 1])
    return x
```

```python
import functools
import math

import numpy as np
import jax
import jax.numpy as jnp
from jax import lax
from jax.experimental import pallas as pl
from jax.experimental.pallas import tpu as pltpu

F32 = jnp.float32
BF16 = jnp.bfloat16

DEPTH = 4
RWKV_HEAD = 64
GN_EPS = 64e-5
DIFF_QK = 128
DIFF_V = 256
REL_BUCKETS = 32
REL_MAX_EXACT = 16
REL_MAX_DIST = 128
CONV_W = 3
LN_EPS = 1e-5
ALPHA = (2 * DEPTH) ** 0.25
NEG_INF = -1e30

LANES = 128
SUBLANES_F32 = 8
SUBLANES_BF16 = 16
VMEM_LIMIT_BYTES = 60000 * 1024

WKV_CHUNK = 64
LORA_PAD = 128
ATTN_TILE = 512
FFN_TILE = 512
FFN_DOWN_KSTEPS = 4


def _cparams(sem):
    return pltpu.CompilerParams(dimension_semantics=sem, vmem_limit_bytes=VMEM_LIMIT_BYTES)


def _dot(a, b):
    return jnp.dot(a, b, preferred_element_type=F32)


def _dot_nt(a, b):
    return lax.dot_general(a, b, (((1,), (1,)), ((), ())), preferred_element_type=F32)


def _dot_tn(a, b):
    return lax.dot_general(a, b, (((0,), (0,)), ((), ())), preferred_element_type=F32)


def _split2(x):
    hi = x.astype(BF16)
    lo = (x - hi.astype(F32)).astype(BF16)
    return hi, lo


def _split3(x):
    hi = x.astype(BF16)
    r1 = x - hi.astype(F32)
    mid = r1.astype(BF16)
    lo = (r1 - mid.astype(F32)).astype(BF16)
    return hi, mid, lo


def _layer_norm(y, g, b):
    mu = jnp.mean(y, axis=-1, keepdims=True)
    yc = y - mu
    var = jnp.mean(yc * yc, axis=-1, keepdims=True)
    return yc * lax.rsqrt(var + LN_EPS) * g + b


def _mm_kernel(a_ref, w_ref, o_ref):
    o_ref[...] = _dot(a_ref[...], w_ref[...]).astype(o_ref.dtype)


def matmul(a, w, out_dtype, tm=1024, tn=1024):
    m, k = a.shape
    n = w.shape[1]
    tm, tn = min(tm, m), min(tn, n)
    assert m % tm == 0 and n % tn == 0
    return pl.pallas_call(
        _mm_kernel,
        grid=(m // tm, n // tn),
        in_specs=[pl.BlockSpec((tm, k), lambda i, j: (i, 0)),
                  pl.BlockSpec((k, tn), lambda i, j: (0, j))],
        out_specs=pl.BlockSpec((tm, tn), lambda i, j: (i, j)),
        out_shape=jax.ShapeDtypeStruct((m, n), out_dtype),
        compiler_params=_cparams(("parallel", "parallel")),
        name="matmul",
    )(a, w)


def _proj_ln_kernel(a_ref, w_ref, res_ref, g_ref, b_ref, xo_ref, xb_ref, acc_ref, *, nk):
    k = pl.program_id(1)

    @pl.when(k == 0)
    def _():
        acc_ref[...] = jnp.zeros_like(acc_ref)

    acc_ref[...] += _dot(a_ref[...], w_ref[...])

    @pl.when(k == nk - 1)
    def _():
        o = _layer_norm(ALPHA * res_ref[...] + acc_ref[...], g_ref[...], b_ref[...])
        xo_ref[...] = o
        xb_ref[...] = o.astype(BF16)


def proj_ln(a, w, res, g, b, tm=512, tk=None):
    m, kdim = a.shape
    d = w.shape[1]
    tm = min(tm, m)
    tk = kdim if tk is None else tk
    assert m % tm == 0 and kdim % tk == 0
    nk = kdim // tk
    return pl.pallas_call(
        functools.partial(_proj_ln_kernel, nk=nk),
        grid=(m // tm, nk),
        in_specs=[pl.BlockSpec((tm, tk), lambda i, k: (i, k)),
                  pl.BlockSpec((tk, d), lambda i, k: (k, 0)),
                  pl.BlockSpec((tm, d), lambda i, k: (i, 0)),
                  pl.BlockSpec((1, d), lambda i, k: (0, 0)),
                  pl.BlockSpec((1, d), lambda i, k: (0, 0))],
        out_specs=[pl.BlockSpec((tm, d), lambda i, k: (i, 0)),
                   pl.BlockSpec((tm, d), lambda i, k: (i, 0))],
        out_shape=[jax.ShapeDtypeStruct((m, d), F32), jax.ShapeDtypeStruct((m, d), BF16)],
        scratch_shapes=[pltpu.VMEM((tm, d), F32)],
        compiler_params=_cparams(("parallel", "arbitrary")),
        name="proj_ln",
    )(a, w, res, g.reshape(1, d), b.reshape(1, d))


def _ffn_up_kernel(x_ref, xh_ref, wu_ref, wg_ref, cw_ref, cb_ref, h_ref, *, tm, seq):
    i = pl.program_id(0)
    x = x_ref[...]
    first = (i * tm) % seq == 0
    halo = jnp.where(first, jnp.zeros_like(xh_ref[...]), xh_ref[...])
    u = _dot(x, wu_ref[...])
    ge = _dot(jnp.concatenate([halo, x], axis=0), wg_ref[...])
    hl = halo.shape[0]
    g0 = ge[hl:, :]
    g1 = pltpu.roll(ge, 1, axis=0)[hl:, :]
    g2 = pltpu.roll(ge, 2, axis=0)[hl:, :]
    gc = cb_ref[...] + g2 * cw_ref[0:1, :] + g1 * cw_ref[1:2, :] + g0 * cw_ref[2:3, :]
    h_ref[...] = (u * jax.nn.gelu(gc)).astype(h_ref.dtype)


def ffn_up(xb, wu, wg, conv_w, conv_b, seq, tm=1024, tn=512):
    m, d = xb.shape
    fp = wu.shape[1]
    tm, tn = min(tm, seq), min(tn, fp)
    hl = SUBLANES_BF16
    assert m % tm == 0 and fp % tn == 0 and seq % tm == 0 and tm % hl == 0
    r = tm // hl
    return pl.pallas_call(
        functools.partial(_ffn_up_kernel, tm=tm, seq=seq),
        grid=(m // tm, fp // tn),
        in_specs=[pl.BlockSpec((tm, d), lambda i, j: (i, 0)),
                  pl.BlockSpec((hl, d), lambda i, j: (jnp.maximum(i * r - 1, 0), 0)),
                  pl.BlockSpec((d, tn), lambda i, j: (0, j)),
                  pl.BlockSpec((d, tn), lambda i, j: (0, j)),
                  pl.BlockSpec((SUBLANES_F32, tn), lambda i, j: (0, j)),
                  pl.BlockSpec((1, tn), lambda i, j: (0, j))],
        out_specs=pl.BlockSpec((tm, tn), lambda i, j: (i, j)),
        out_shape=jax.ShapeDtypeStruct((m, fp), BF16),
        compiler_params=_cparams(("parallel", "parallel")),
        name="ffn_up",
    )(xb, xb, wu, wg, conv_w, conv_b)


def _bucket_thresholds():
    n = np.arange(1, 4 * REL_MAX_DIST, dtype=np.float64)
    large = REL_MAX_EXACT + np.floor(np.log(n / REL_MAX_EXACT) / math.log(REL_MAX_DIST / REL_MAX_EXACT)
                                     * (REL_BUCKETS - REL_MAX_EXACT)).astype(np.int64)
    bucket = np.where(n < REL_MAX_EXACT, n.astype(np.int64), np.minimum(large, REL_BUCKETS - 1))
    thr = [int(n[i]) for i in range(len(n)) if i == 0 or bucket[i] != bucket[i - 1]]
    assert bucket[0] == 1 and len(thr) == REL_BUCKETS - 1 and np.all(np.diff(bucket) >= 0)
    return thr


_BUCKET_THR = _bucket_thresholds()
REL_FAR = _BUCKET_THR[-1]


def _bias_tile_kernel(rb_ref, o_ref, *, tq, tk):
    h = pl.program_id(0)
    w = pl.program_id(1)
    row = lax.broadcasted_iota(jnp.int32, (tq, tk), 0)
    col = lax.broadcasted_iota(jnp.int32, (tq, tk), 1)
    dist = w * tk + row - col
    val = jnp.full((tq, tk), rb_ref[0, h], F32)
    for m, t in enumerate(_BUCKET_THR):
        val = jnp.where(dist >= t, rb_ref[m + 1, h], val)
    o_ref[0, 0] = jnp.where(dist >= 0, val, NEG_INF)


def bias_tiles(rel_bias, tq, tk):
    nh = rel_bias.shape[1]
    return pl.pallas_call(
        functools.partial(_bias_tile_kernel, tq=tq, tk=tk),
        grid=(nh, 2),
        in_specs=[pl.BlockSpec(memory_space=pltpu.SMEM)],
        out_specs=pl.BlockSpec((1, 1, tq, tk), lambda h, w: (h, w, 0, 0)),
        out_shape=jax.ShapeDtypeStruct((nh, 2, tq, tk), F32),
        compiler_params=_cparams(("parallel", "parallel")),
        name="bias_tiles",
    )(rel_bias)


def _diff_attn_kernel(rb_ref, q_ref, k_ref, v_ref, bias_ref, lam_ref, sg_ref, o_ref,
                      acc0_ref, acc1_ref, *, tq, tk, lambda_init):
    h = pl.program_id(1)
    qi = pl.program_id(2)
    scale = DIFF_QK ** -0.5
    q = q_ref[...].astype(F32) * scale
    q0 = q[:, :DIFF_QK].astype(BF16)
    q1 = q[:, DIFF_QK:].astype(BF16)
    acc0_ref[...] = jnp.zeros_like(acc0_ref)
    acc1_ref[...] = jnp.zeros_like(acc1_ref)

    def update(s, m, l, acc_ref, v):
        m_new = jnp.maximum(m, jnp.max(s, axis=-1, keepdims=True))
        alpha = jnp.exp(m - m_new)
        p = jnp.exp(s - m_new)
        l_new = alpha * l + jnp.sum(p, axis=-1, keepdims=True)
        acc_ref[...] = alpha * acc_ref[...] + _dot(p.astype(BF16), v)
        return m_new, l_new

    def step(ki, carry, bias):
        m0, l0, m1, l1 = carry
        start = pl.multiple_of(ki * tk, tk)
        kb = k_ref[pl.ds(start, tk), :]
        vb = v_ref[pl.ds(start, tk), :]
        s0 = _dot_nt(q0, kb[:, :DIFF_QK]) + bias
        s1 = _dot_nt(q1, kb[:, DIFF_QK:]) + bias
        m0, l0 = update(s0, m0, l0, acc0_ref, vb)
        m1, l1 = update(s1, m1, l1, acc1_ref, vb)
        return m0, l0, m1, l1

    init = (jnp.full((tq, 1), NEG_INF, F32), jnp.zeros((tq, 1), F32),
            jnp.full((tq, 1), NEG_INF, F32), jnp.zeros((tq, 1), F32))
    far_bias = rb_ref[REL_BUCKETS - 1, h]
    carry = lax.fori_loop(0, jnp.maximum(qi - 1, 0), lambda ki, c: step(ki, c, far_bias), init)
    carry = lax.cond(qi >= 1, lambda c: step(qi - 1, c, bias_ref[0, 1]), lambda c: c, carry)
    m0, l0, m1, l1 = step(qi, carry, bias_ref[0, 0])

    lam = lam_ref[...]
    lam_full = (jnp.exp(jnp.sum(lam[0:1, :] * lam[1:2, :], axis=-1, keepdims=True))
                - jnp.exp(jnp.sum(lam[2:3, :] * lam[3:4, :], axis=-1, keepdims=True)) + lambda_init)
    o = acc0_ref[...] / l0 - lam_full * (acc1_ref[...] / l1)
    o = o * lax.rsqrt(jnp.mean(o * o, axis=-1, keepdims=True) + LN_EPS) * sg_ref[...] * (1.0 - lambda_init)
    o_ref[...] = o.astype(o_ref.dtype)


def diff_attn(qkv, btiles, rel_bias, lam, subln_g, batch, seq, lambda_init, tq=512):
    m, d3 = qkv.shape
    d = d3 // 3
    nh = d // DIFF_V
    tq = min(tq, seq)
    tk = tq
    assert seq % tq == 0 and tk >= REL_FAR and btiles.shape == (nh, 2, tq, tk)
    nq = seq // tq
    return pl.pallas_call(
        functools.partial(_diff_attn_kernel, tq=tq, tk=tk, lambda_init=lambda_init),
        grid=(batch, nh, nq),
        in_specs=[pl.BlockSpec(memory_space=pltpu.SMEM),
                  pl.BlockSpec((tq, DIFF_V), lambda b, h, i: (b * nq + i, h)),
                  pl.BlockSpec((seq, DIFF_V), lambda b, h, i: (b, nh + h)),
                  pl.BlockSpec((seq, DIFF_V), lambda b, h, i: (b, 2 * nh + h)),
                  pl.BlockSpec((1, 2, tq, tk), lambda b, h, i: (h, 0, 0, 0)),
                  pl.BlockSpec((4, DIFF_QK), lambda b, h, i: (0, 0)),
                  pl.BlockSpec((1, DIFF_V), lambda b, h, i: (0, 0))],
        out_specs=pl.BlockSpec((tq, DIFF_V), lambda b, h, i: (b * nq + i, h)),
        out_shape=jax.ShapeDtypeStruct((m, d), BF16),
        scratch_shapes=[pltpu.VMEM((tq, DIFF_V), F32), pltpu.VMEM((tq, DIFF_V), F32)],
        compiler_params=_cparams(("parallel", "parallel", "arbitrary")),
        name="diff_attn",
    )(rel_bias, qkv, qkv, qkv, btiles, lam, subln_g.reshape(1, DIFF_V))


def _shift_delta(x_ref, xh_ref, tm, seq, i):
    x = x_ref[...]
    first = (i * tm) % seq == 0
    prev = jnp.where(first, jnp.zeros((1, x.shape[1]), F32), xh_ref[SUBLANES_F32 - 1:SUBLANES_F32, :])
    row = lax.broadcasted_iota(jnp.int32, x.shape, 0)
    return x, jnp.where(row == 0, prev, pltpu.roll(x, 1, axis=0)) - x


def _rwkv_rkv_kernel(x_ref, xh_ref, mu_ref, w_ref, o_ref, *, tm, seq):
    x, xx = _shift_delta(x_ref, xh_ref, tm, seq, pl.program_id(1))
    xin = (x + xx * mu_ref[0]).astype(BF16)
    o_ref[0] = _dot(xin, w_ref[0])


def rwkv_rkv(x, mu3, wrkv, seq, tm=512):
    m, d = x.shape
    tm = min(tm, seq)
    assert m % tm == 0 and seq % tm == 0
    r = tm // SUBLANES_F32
    return pl.pallas_call(
        functools.partial(_rwkv_rkv_kernel, tm=tm, seq=seq),
        grid=(3, m // tm),
        in_specs=[pl.BlockSpec((tm, d), lambda g, i: (i, 0)),
                  pl.BlockSpec((SUBLANES_F32, d), lambda g, i: (jnp.maximum(i * r - 1, 0), 0)),
                  pl.BlockSpec((1, 1, d), lambda g, i: (g, 0, 0)),
                  pl.BlockSpec((1, d, d), lambda g, i: (g, 0, 0))],
        out_specs=pl.BlockSpec((1, tm, d), lambda g, i: (g, i, 0)),
        out_shape=jax.ShapeDtypeStruct((3, m, d), F32),
        compiler_params=_cparams(("parallel", "parallel")),
        name="rwkv_rkv",
    )(x, x, mu3, wrkv)


def _rwkv_lora_kernel(x_ref, xh_ref, mu_ref, wl_ref, hid_ref, *, tm, seq):
    x, xx = _shift_delta(x_ref, xh_ref, tm, seq, pl.program_id(0))

    def lora(j):
        xin = (x + xx * mu_ref[j:j + 1, :]).astype(BF16)
        return _dot(xin, wl_ref[:, j * LORA_PAD:(j + 1) * LORA_PAD])

    hid_ref[:, 0 * LORA_PAD:1 * LORA_PAD] = jnp.tanh(lora(0)).astype(BF16)
    hid_ref[:, 1 * LORA_PAD:2 * LORA_PAD] = lora(1).astype(BF16)
    hid_ref[:, 2 * LORA_PAD:3 * LORA_PAD] = lora(2).astype(BF16)
    hid_ref[:, 3 * LORA_PAD:4 * LORA_PAD] = jax.nn.sigmoid(lora(3)).astype(BF16)


def rwkv_lora(x, mu4, wl, seq, tm=512):
    m, d = x.shape
    tm = min(tm, seq)
    assert m % tm == 0 and seq % tm == 0
    r = tm // SUBLANES_F32
    return pl.pallas_call(
        functools.partial(_rwkv_lora_kernel, tm=tm, seq=seq),
        grid=(m // tm,),
        in_specs=[pl.BlockSpec((tm, d), lambda i: (i, 0)),
                  pl.BlockSpec((SUBLANES_F32, d), lambda i: (jnp.maximum(i * r - 1, 0), 0)),
                  pl.BlockSpec((SUBLANES_F32, d), lambda i: (0, 0)),
                  pl.BlockSpec((d, 4 * LORA_PAD), lambda i: (0, 0))],
        out_specs=pl.BlockSpec((tm, 4 * LORA_PAD), lambda i: (i, 0)),
        out_shape=jax.ShapeDtypeStruct((m, 4 * LORA_PAD), BF16),
        compiler_params=_cparams(("parallel",)),
        name="rwkv_lora",
    )(x, x, mu4, wl)


def _softplus(x):
    return jnp.maximum(x, 0.0) + jnp.log(1.0 + jnp.exp(-jnp.abs(x)))


def _wkv_kernel(*refs, use_vres, tb):
    if use_vres:
        (r_ref, k_ref, v_ref, vf_ref, hid_ref, w2_ref, a2_ref, v2_ref, g2_ref, vec_ref, z_ref, s_ref) = refs
    else:
        (r_ref, k_ref, v_ref, hid_ref, w2_ref, a2_ref, g2_ref, vec_ref, z_ref, s_ref) = refs
    n = RWKV_HEAD
    lc = WKV_CHUNK
    lg = lc.bit_length() - 1

    @pl.when(pl.program_id(2) == 0)
    def _():
        s_ref[...] = jnp.zeros_like(s_ref)

    r, k, v = r_ref[0], k_ref[0], v_ref[0]
    hid = hid_ref[...]
    vec = vec_ref[...]
    w0, a0, v0, k_k, k_a, lnx_g, lnx_b, r_k = (vec[j:j + 1, :] for j in range(8))

    lane = lax.broadcasted_iota(jnp.int32, (1, LANES), 1)
    head_masks = (lane < n, lane >= n)
    pr = lax.broadcasted_iota(jnp.int32, (LANES, LANES), 0)
    pc = lax.broadcasted_iota(jnp.int32, (LANES, LANES), 1)
    same_head = (pr >= n) == (pc >= n)
    p_sum = jnp.where(same_head, 1.0, 0.0).astype(BF16)
    p_mean = jnp.where(same_head, 1.0 / n, 0.0).astype(BF16)

    def head_reduce(x, p):
        hi, lo = _split2(x)
        return _dot(hi, p) + _dot(lo, p)

    wlog = -_softplus(-(w0 + _dot(hid[:, 0 * LORA_PAD:1 * LORA_PAD], w2_ref[...]))) - 0.5
    ld = -jnp.exp(wlog)
    a = jax.nn.sigmoid(a0 + _dot(hid[:, 1 * LORA_PAD:2 * LORA_PAD], a2_ref[...]))
    g = _dot(hid[:, 3 * LORA_PAD:4 * LORA_PAD], g2_ref[...])
    if use_vres:
        v = v + (vf_ref[...] - v) * jax.nn.sigmoid(v0 + _dot(hid[:, 2 * LORA_PAD:3 * LORA_PAD], v2_ref[...]))
    kk = k * k_k
    kk = kk / jnp.maximum(jnp.sqrt(head_reduce(kk * kk, p_sum)), 1e-12)
    k2 = k * (1.0 + (a - 1.0) * k_a)

    ri = lax.broadcasted_iota(jnp.int32, (tb, tb), 0)
    ci = lax.broadcasted_iota(jnp.int32, (tb, tb), 1)
    same_chunk = (ri >> lg) == (ci >> lg)
    incl = same_chunk & (ri >= ci)
    strict = same_chunk & (ri > ci)
    tril = jnp.where(incl, 1.0, 0.0).astype(BF16)
    hi, mid, lo = _split3(ld)
    cs = _dot(tril, hi) + _dot(tril, mid) + _dot(tril, lo)
    e_pos = jnp.exp(cs)
    e_neg = jnp.exp(-cs)
    rt = r * e_pos
    kt = k2 * e_neg
    bt = kk * a * e_neg
    at = -kk * jnp.exp(cs - ld)

    btb, ktb, vb = bt.astype(BF16), kt.astype(BF16), v.astype(BF16)
    rhs = jnp.concatenate([btb, ktb], axis=0)
    w_all = at
    uv_all = jnp.zeros((tb, LANES), F32)
    y0 = jnp.zeros((tb, LANES), F32)
    arb = []
    for hm in head_masks:
        at_h = jnp.where(hm, at, 0.0).astype(BF16)
        rt_h = jnp.where(hm, rt, 0.0).astype(BF16)
        amat = _dot_nt(jnp.concatenate([at_h, rt_h], axis=0), rhs)
        a_ab = jnp.where(strict, amat[:tb, :tb], 0.0)
        a_ak = jnp.where(strict, amat[:tb, tb:], 0.0).astype(BF16)
        a_rb = jnp.where(incl, amat[tb:, :tb], 0.0).astype(BF16)
        a_rk = jnp.where(incl, amat[tb:, tb:], 0.0).astype(BF16)
        x = jnp.where(ri == ci, 1.0, 0.0) + jnp.where(((ri >> 1) == (ci >> 1)), a_ab, 0.0)
        for lb in range(1, lg):
            e = jnp.where(((ri >> (lb + 1)) == (ci >> (lb + 1))) & (((ri >> lb) & 1) == 1) & (((ci >> lb) & 1) == 0),
                          a_ab, 0.0).astype(BF16)
            xb = x.astype(BF16)
            x = x + _dot(_dot(xb, e).astype(BF16), xb)
        z = jnp.where(ri == ci, 0.0, x).astype(BF16)
        q = jnp.where(hm, _dot(a_ak, vb), 0.0)
        w_all = w_all + _dot(z, at_h)
        uv_all = uv_all + q + _dot(z, q.astype(BF16))
        y0 = y0 + jnp.where(hm, _dot(a_rk, vb), 0.0)
        arb.append(a_rb)

    s = s_ref[...]
    wb, rtb = w_all.astype(BF16), rt.astype(BF16)
    us, yrs = [], []
    for c in range(tb // lc):
        sl = slice(c * lc, (c + 1) * lc)
        sb = s.astype(BF16)
        u_c = _dot_nt(wb[sl], sb) + uv_all[sl]
        yrs.append(_dot_nt(rtb[sl], sb))
        ds = _dot_tn(jnp.concatenate([u_c.astype(BF16), vb[sl]], axis=0),
                     jnp.concatenate([btb[sl], ktb[sl]], axis=0))
        s = (s + jnp.where(same_head, ds, 0.0)) * e_pos[(c + 1) * lc - 1:(c + 1) * lc, :]
        us.append(u_c)
    s_ref[...] = s
    ub = jnp.concatenate(us, axis=0).astype(BF16)
    y = jnp.concatenate(yrs, axis=0) + y0
    for hm, a_rb in zip(head_masks, arb):
        y = y + jnp.where(hm, _dot(a_rb, ub), 0.0)

    yc = y - head_reduce(y, p_mean)
    yn = yc * lax.rsqrt(head_reduce(yc * yc, p_mean) + GN_EPS)
    bonus = head_reduce(r * k2 * r_k, p_sum) * v
    z_ref[...] = ((yn * lnx_g + lnx_b + bonus) * g).astype(z_ref.dtype)


def wkv(rkv, vfirst, hid, w2, a2, v2, g2, vec, batch, seq, tb=256):
    _, m, d = rkv.shape
    tb = min(tb, seq)
    assert seq % tb == 0 and tb % WKV_CHUNK == 0 and d % LANES == 0
    nt = seq // tb
    use_vres = vfirst is not None
    tile = lambda b, p, t: (b * nt + t, p)
    lora_spec = pl.BlockSpec((LORA_PAD, LANES), lambda b, p, t: (0, p))
    in_specs = [pl.BlockSpec((1, tb, LANES), lambda b, p, t: (0, b * nt + t, p)),
                pl.BlockSpec((1, tb, LANES), lambda b, p, t: (1, b * nt + t, p)),
                pl.BlockSpec((1, tb, LANES), lambda b, p, t: (2, b * nt + t, p))]
    args = [rkv, rkv, rkv]
    if use_vres:
        in_specs.append(pl.BlockSpec((tb, LANES), tile))
        args.append(vfirst)
    in_specs.append(pl.BlockSpec((tb, 4 * LORA_PAD), lambda b, p, t: (b * nt + t, 0)))
    args.append(hid)
    for wgt in ((w2, a2, v2, g2) if use_vres else (w2, a2, g2)):
        in_specs.append(lora_spec)
        args.append(wgt)
    in_specs.append(pl.BlockSpec((SUBLANES_F32, LANES), lambda b, p, t: (0, p)))
    args.append(vec)
    return pl.pallas_call(
        functools.partial(_wkv_kernel, use_vres=use_vres, tb=tb),
        grid=(batch, d // LANES, nt),
        in_specs=in_specs,
        out_specs=pl.BlockSpec((tb, LANES), tile),
        out_shape=jax.ShapeDtypeStruct((m, d), BF16),
        scratch_shapes=[pltpu.VMEM((LANES, LANES), F32)],
        compiler_params=_cparams(("parallel", "parallel", "arbitrary")),
        name="wkv",
    )(*args)


def _pad_to(x, axis, size):
    pad = [(0, 0)] * x.ndim
    pad[axis] = (0, size - x.shape[axis])
    return jnp.pad(x, pad)


def _rwkv_layer(x, xshape, j, p, v_first):
    batch, seq, d = xshape
    mu = p["rw_mu"][j]
    mu3 = jnp.stack([mu[0], mu[2], mu[3]])[:, None, :]
    mu4 = _pad_to(jnp.stack([mu[1], mu[4], mu[3], mu[5]]), 0, SUBLANES_F32)
    use_vres = j > 0
    v1 = p["rw_v1"][j - 1] if use_vres else jnp.zeros((d, LORA_PAD), F32)
    wl = jnp.concatenate([_pad_to(w, 1, LORA_PAD) for w in (p["rw_w1"][j], p["rw_a1"][j], v1, p["rw_g1"][j])],
                         axis=1).astype(BF16)
    up = lambda w: _pad_to(w, 0, LORA_PAD).astype(BF16)
    vec = jnp.stack([p["rw_w0"][j], p["rw_a0"][j], p["rw_v0"][j - 1] if use_vres else jnp.zeros((d,), F32),
                     p["rw_k_k"][j], p["rw_k_a"][j], p["rw_lnx_g"][j], p["rw_lnx_b"][j], p["rw_r_k"][j].reshape(d)])
    rkv = rwkv_rkv(x, mu3, p["rw_w_rkv"][j].astype(BF16), seq)
    hid = rwkv_lora(x, mu4, wl, seq)
    if not use_vres:
        v_first = rkv[2]
    z = wkv(rkv, v_first if use_vres else None, hid, up(p["rw_w2"][j]), up(p["rw_a2"][j]),
            up(p["rw_v2"][j - 1]) if use_vres else None, up(p["rw_g2"][j]), vec, batch, seq)
    return z, p["rw_w_o"][j].astype(BF16), v_first


def _attn_layer(xb, xshape, i, j, p, btiles, tq):
    batch, seq, d = xshape
    lambda_init = 0.8 - 0.6 * math.exp(-0.3 * i)
    qkv = matmul(xb, p["da_w_qkv"][j].astype(BF16), BF16)
    o = diff_attn(qkv, btiles, p["rel_bias"], p["da_lam"][j], p["da_subln_g"][j], batch, seq, lambda_init, tq=tq)
    return o, p["da_w_o"][j].astype(BF16)


def kernel(x, ln_g, ln_b, rw_mu, rw_w_rkv, rw_w0, rw_w1, rw_w2, rw_a0, rw_a1, rw_a2, rw_v0, rw_v1, rw_v2, rw_g1, rw_g2, rw_k_k, rw_k_a, rw_r_k, rw_lnx_g, rw_lnx_b, rw_w_o, da_w_qkv, da_lam, da_subln_g, da_w_o, rel_bias, ff_w_up, ff_conv_w, ff_conv_b, ff_w_down):
    p = dict(rw_mu=rw_mu, rw_w_rkv=rw_w_rkv, rw_w0=rw_w0, rw_w1=rw_w1, rw_w2=rw_w2, rw_a0=rw_a0, rw_a1=rw_a1,
             rw_a2=rw_a2, rw_v0=rw_v0, rw_v1=rw_v1, rw_v2=rw_v2, rw_g1=rw_g1, rw_g2=rw_g2, rw_k_k=rw_k_k,
             rw_k_a=rw_k_a, rw_r_k=rw_r_k, rw_lnx_g=rw_lnx_g, rw_lnx_b=rw_lnx_b, rw_w_o=rw_w_o,
             da_w_qkv=da_w_qkv, da_lam=da_lam, da_subln_g=da_subln_g, da_w_o=da_w_o, rel_bias=rel_bias)
    batch, seq, d = x.shape
    xshape = (batch, seq, d)
    f = ff_conv_b.shape[1]
    fp = -(-f // FFN_TILE) * FFN_TILE
    tq = min(ATTN_TILE, seq)
    btiles = bias_tiles(rel_bias, tq, tq)
    xf = x.reshape(batch * seq, d)
    xb = xf.astype(BF16)
    v_first = None
    for i in range(DEPTH):
        j = i // 2
        if i % 2 == 0:
            z, w_o, v_first = _rwkv_layer(xf, xshape, j, p, v_first)
        else:
            z, w_o = _attn_layer(xb, xshape, i, j, p, btiles, tq)
        xf, xb = proj_ln(z, w_o, xf, ln_g[i, 0], ln_b[i, 0])
        wu = _pad_to(ff_w_up[i][:, :f].astype(BF16), 1, fp)
        wg = _pad_to(ff_w_up[i][:, f:].astype(BF16), 1, fp)
        cw = _pad_to(_pad_to(ff_conv_w[i], 1, fp), 0, SUBLANES_F32)
        cb = _pad_to(ff_conv_b[i][None, :], 1, fp)
        h = ffn_up(xb, wu, wg, cw, cb, seq, tn=FFN_TILE)
        wd = _pad_to(ff_w_down[i].astype(BF16), 0, fp)
        xf, xb = proj_ln(h, wd, xf, ln_g[i, 1], ln_b[i, 1], tk=fp // FFN_DOWN_KSTEPS)
    return xf.reshape(batch, seq, d)
```

```python
import functools
import math

import numpy as np
import jax
import jax.numpy as jnp
from jax import lax
from jax.experimental import pallas as pl
from jax.experimental.pallas import tpu as pltpu

F32 = jnp.float32
BF16 = jnp.bfloat16

DEPTH = 4
RWKV_HEAD = 64
GN_EPS = 64e-5
DIFF_QK = 128
DIFF_V = 256
REL_BUCKETS = 32
REL_MAX_EXACT = 16
REL_MAX_DIST = 128
CONV_W = 3
LN_EPS = 1e-5
ALPHA = (2 * DEPTH) ** 0.25
NEG_INF = -1e30

LANES = 128
SUBLANES_F32 = 8
SUBLANES_BF16 = 16
VMEM_LIMIT_BYTES = 60000 * 1024

WKV_CHUNK = 64
LORA_PAD = 128
ATTN_TILE = 512
FFN_TILE = 512
FFN_DOWN_KSTEPS = 4


def _cparams(sem):
    return pltpu.CompilerParams(dimension_semantics=sem, vmem_limit_bytes=VMEM_LIMIT_BYTES)


def _dot(a, b):
    return jnp.dot(a, b, preferred_element_type=F32)


def _dot_nt(a, b):
    return lax.dot_general(a, b, (((1,), (1,)), ((), ())), preferred_element_type=F32)


def _dot_tn(a, b):
    return lax.dot_general(a, b, (((0,), (0,)), ((), ())), preferred_element_type=F32)


def _split2(x):
    hi = x.astype(BF16)
    lo = (x - hi.astype(F32)).astype(BF16)
    return hi, lo


def _split3(x):
    hi = x.astype(BF16)
    r1 = x - hi.astype(F32)
    mid = r1.astype(BF16)
    lo = (r1 - mid.astype(F32)).astype(BF16)
    return hi, mid, lo


def _layer_norm(y, g, b):
    mu = jnp.mean(y, axis=-1, keepdims=True)
    yc = y - mu
    var = jnp.mean(yc * yc, axis=-1, keepdims=True)
    return yc * lax.rsqrt(var + LN_EPS) * g + b


def _mm_kernel(a_ref, w_ref, o_ref):
    o_ref[...] = _dot(a_ref[...], w_ref[...]).astype(o_ref.dtype)


def matmul(a, w, out_dtype, tm=1024, tn=1024):
    m, k = a.shape
    n = w.shape[1]
    tm, tn = min(tm, m), min(tn, n)
    assert m % tm == 0 and n % tn == 0
    return pl.pallas_call(
        _mm_kernel,
        grid=(m // tm, n // tn),
        in_specs=[pl.BlockSpec((tm, k), lambda i, j: (i, 0)),
                  pl.BlockSpec((k, tn), lambda i, j: (0, j))],
        out_specs=pl.BlockSpec((tm, tn), lambda i, j: (i, j)),
        out_shape=jax.ShapeDtypeStruct((m, n), out_dtype),
        compiler_params=_cparams(("parallel", "parallel")),
        name="matmul",
    )(a, w)


def _proj_ln_kernel(a_ref, w_ref, res_ref, g_ref, b_ref, xo_ref, xb_ref, acc_ref, *, nk):
    k = pl.program_id(1)

    @pl.when(k == 0)
    def _():
        acc_ref[...] = jnp.zeros_like(acc_ref)

    acc_ref[...] += _dot(a_ref[...], w_ref[...])

    @pl.when(k == nk - 1)
    def _():
        o = _layer_norm(ALPHA * res_ref[...] + acc_ref[...], g_ref[...], b_ref[...])
        xo_ref[...] = o
        xb_ref[...] = o.astype(BF16)


def proj_ln(a, w, res, g, b, tm=512, tk=None):
    m, kdim = a.shape
    d = w.shape[1]
    tm = min(tm, m)
    tk = kdim if tk is None else tk
    assert m % tm == 0 and kdim % tk == 0
    nk = kdim // tk
    return pl.pallas_call(
        functools.partial(_proj_ln_kernel, nk=nk),
        grid=(m // tm, nk),
        in_specs=[pl.BlockSpec((tm, tk), lambda i, k: (i, k)),
                  pl.BlockSpec((tk, d), lambda i, k: (k, 0)),
                  pl.BlockSpec((tm, d), lambda i, k: (i, 0)),
                  pl.BlockSpec((1, d), lambda i, k: (0, 0)),
                  pl.BlockSpec((1, d), lambda i, k: (0, 0))],
        out_specs=[pl.BlockSpec((tm, d), lambda i, k: (i, 0)),
                   pl.BlockSpec((tm, d), lambda i, k: (i, 0))],
        out_shape=[jax.ShapeDtypeStruct((m, d), F32), jax.ShapeDtypeStruct((m, d), BF16)],
        scratch_shapes=[pltpu.VMEM((tm, d), F32)],
        compiler_params=_cparams(("parallel", "arbitrary")),
        name="proj_ln",
    )(a, w, res, g.reshape(1, d), b.reshape(1, d))


def _ffn_up_kernel(x_ref, xh_ref, wu_ref, wg_ref, cw_ref, cb_ref, h_ref, *, tm, seq):
    i = pl.program_id(0)
    x = x_ref[...]
    first = (i * tm) % seq == 0
    halo = jnp.where(first, jnp.zeros_like(xh_ref[...]), xh_ref[...])
    u = _dot(x, wu_ref[...])
    ge = _dot(jnp.concatenate([halo, x], axis=0), wg_ref[...])
    hl = halo.shape[0]
    g0 = ge[hl:, :]
    g1 = pltpu.roll(ge, 1, axis=0)[hl:, :]
    g2 = pltpu.roll(ge, 2, axis=0)[hl:, :]
    gc = cb_ref[...] + g2 * cw_ref[0:1, :] + g1 * cw_ref[1:2, :] + g0 * cw_ref[2:3, :]
    h_ref[...] = (u * jax.nn.gelu(gc)).astype(h_ref.dtype)


def ffn_up(xb, wu, wg, conv_w, conv_b, seq, tm=1024, tn=512):
    m, d = xb.shape
    fp = wu.shape[1]
    tm, tn = min(tm, seq), min(tn, fp)
    hl = SUBLANES_BF16
    assert m % tm == 0 and fp % tn == 0 and seq % tm == 0 and tm % hl == 0
    r = tm // hl
    return pl.pallas_call(
        functools.partial(_ffn_up_kernel, tm=tm, seq=seq),
        grid=(m // tm, fp // tn),
        in_specs=[pl.BlockSpec((tm, d), lambda i, j: (i, 0)),
                  pl.BlockSpec((hl, d), lambda i, j: (jnp.maximum(i * r - 1, 0), 0)),
                  pl.BlockSpec((d, tn), lambda i, j: (0, j)),
                  pl.BlockSpec((d, tn), lambda i, j: (0, j)),
                  pl.BlockSpec((SUBLANES_F32, tn), lambda i, j: (0, j)),
                  pl.BlockSpec((1, tn), lambda i, j: (0, j))],
        out_specs=pl.BlockSpec((tm, tn), lambda i, j: (i, j)),
        out_shape=jax.ShapeDtypeStruct((m, fp), BF16),
        compiler_params=_cparams(("parallel", "parallel")),
        name="ffn_up",
    )(xb, xb, wu, wg, conv_w, conv_b)


def _bucket_thresholds():
    n = np.arange(1, 4 * REL_MAX_DIST, dtype=np.float64)
    large = REL_MAX_EXACT + np.floor(np.log(n / REL_MAX_EXACT) / math.log(REL_MAX_DIST / REL_MAX_EXACT)
                                     * (REL_BUCKETS - REL_MAX_EXACT)).astype(np.int64)
    bucket = np.where(n < REL_MAX_EXACT, n.astype(np.int64), np.minimum(large, REL_BUCKETS - 1))
    thr = [int(n[i]) for i in range(len(n)) if i == 0 or bucket[i] != bucket[i - 1]]
    assert bucket[0] == 1 and len(thr) == REL_BUCKETS - 1 and np.all(np.diff(bucket) >= 0)
    return thr


_BUCKET_THR = _bucket_thresholds()
REL_FAR = _BUCKET_THR[-1]


def _bias_tile_kernel(rb_ref, o_ref, *, tq, tk):
    h = pl.program_id(0)
    w = pl.program_id(1)
    row = lax.broadcasted_iota(jnp.int32, (tq, tk), 0)
    col = lax.broadcasted_iota(jnp.int32, (tq, tk), 1)
    dist = w * tk + row - col
    val = jnp.full((tq, tk), rb_ref[0, h], F32)
    for m, t in enumerate(_BUCKET_THR):
        val = jnp.where(dist >= t, rb_ref[m + 1, h], val)
    o_ref[0, 0] = jnp.where(dist >= 0, val, NEG_INF)


def bias_tiles(rel_bias, tq, tk):
    nh = rel_bias.shape[1]
    return pl.pallas_call(
        functools.partial(_bias_tile_kernel, tq=tq, tk=tk),
        grid=(nh, 2),
        in_specs=[pl.BlockSpec(memory_space=pltpu.SMEM)],
        out_specs=pl.BlockSpec((1, 1, tq, tk), lambda h, w: (h, w, 0, 0)),
        out_shape=jax.ShapeDtypeStruct((nh, 2, tq, tk), F32),
        compiler_params=_cparams(("parallel", "parallel")),
        name="bias_tiles",
    )(rel_bias)


def _diff_attn_kernel(rb_ref, q_ref, k_ref, v_ref, bias_ref, lam_ref, sg_ref, o_ref,
                      acc0_ref, acc1_ref, *, tq, tk, lambda_init):
    h = pl.program_id(1)
    qi = pl.program_id(2)
    scale = DIFF_QK ** -0.5
    q = q_ref[...].astype(F32) * scale
    q0 = q[:, :DIFF_QK].astype(BF16)
    q1 = q[:, DIFF_QK:].astype(BF16)
    acc0_ref[...] = jnp.zeros_like(acc0_ref)
    acc1_ref[...] = jnp.zeros_like(acc1_ref)

    def update(s, m, l, acc_ref, v):
        m_new = jnp.maximum(m, jnp.max(s, axis=-1, keepdims=True))
        alpha = jnp.exp(m - m_new)
        p = jnp.exp(s - m_new)
        l_new = alpha * l + jnp.sum(p, axis=-1, keepdims=True)
        acc_ref[...] = alpha * acc_ref[...] + _dot(p.astype(BF16), v)
        return m_new, l_new

    def step(ki, carry, bias):
        m0, l0, m1, l1 = carry
        start = pl.multiple_of(ki * tk, tk)
        kb = k_ref[pl.ds(start, tk), :]
        vb = v_ref[pl.ds(start, tk), :]
        s0 = _dot_nt(q0, kb[:, :DIFF_QK]) + bias
        s1 = _dot_nt(q1, kb[:, DIFF_QK:]) + bias
        m0, l0 = update(s0, m0, l0, acc0_ref, vb)
        m1, l1 = update(s1, m1, l1, acc1_ref, vb)
        return m0, l0, m1, l1

    init = (jnp.full((tq, 1), NEG_INF, F32), jnp.zeros((tq, 1), F32),
            jnp.full((tq, 1), NEG_INF, F32), jnp.zeros((tq, 1), F32))
    far_bias = rb_ref[REL_BUCKETS - 1, h]
    carry = lax.fori_loop(0, jnp.maximum(qi - 1, 0), lambda ki, c: step(ki, c, far_bias), init)
    carry = lax.cond(qi >= 1, lambda c: step(qi - 1, c, bias_ref[0, 1]), lambda c: c, carry)
    m0, l0, m1, l1 = step(qi, carry, bias_ref[0, 0])

    lam = lam_ref[...]
    lam_full = (jnp.exp(jnp.sum(lam[0:1, :] * lam[1:2, :], axis=-1, keepdims=True))
                - jnp.exp(jnp.sum(lam[2:3, :] * lam[3:4, :], axis=-1, keepdims=True)) + lambda_init)
    o = acc0_ref[...] / l0 - lam_full * (acc1_ref[...] / l1)
    o = o * lax.rsqrt(jnp.mean(o * o, axis=-1, keepdims=True) + LN_EPS) * sg_ref[...] * (1.0 - lambda_init)
    o_ref[...] = o.astype(o_ref.dtype)


def diff_attn(qkv, btiles, rel_bias, lam, subln_g, batch, seq, lambda_init, tq=512):
    m, d3 = qkv.shape
    d = d3 // 3
    nh = d // DIFF_V
    tq = min(tq, seq)
    tk = tq
    assert seq % tq == 0 and tk >= REL_FAR and btiles.shape == (nh, 2, tq, tk)
    nq = seq // tq
    return pl.pallas_call(
        functools.partial(_diff_attn_kernel, tq=tq, tk=tk, lambda_init=lambda_init),
        grid=(batch, nh, nq),
        in_specs=[pl.BlockSpec(memory_space=pltpu.SMEM),
                  pl.BlockSpec((tq, DIFF_V), lambda b, h, i: (b * nq + i, h)),
                  pl.BlockSpec((seq, DIFF_V), lambda b, h, i: (b, nh + h)),
                  pl.BlockSpec((seq, DIFF_V), lambda b, h, i: (b, 2 * nh + h)),
                  pl.BlockSpec((1, 2, tq, tk), lambda b, h, i: (h, 0, 0, 0)),
                  pl.BlockSpec((4, DIFF_QK), lambda b, h, i: (0, 0)),
                  pl.BlockSpec((1, DIFF_V), lambda b, h, i: (0, 0))],
        out_specs=pl.BlockSpec((tq, DIFF_V), lambda b, h, i: (b * nq + i, h)),
        out_shape=jax.ShapeDtypeStruct((m, d), BF16),
        scratch_shapes=[pltpu.VMEM((tq, DIFF_V), F32), pltpu.VMEM((tq, DIFF_V), F32)],
        compiler_params=_cparams(("parallel", "parallel", "arbitrary")),
        name="diff_attn",
    )(rel_bias, qkv, qkv, qkv, btiles, lam, subln_g.reshape(1, DIFF_V))


def _shift_delta(x_ref, xh_ref, tm, seq, i):
    x = x_ref[...]
    first = (i * tm) % seq == 0
    prev = jnp.where(first, jnp.zeros((1, x.shape[1]), F32), xh_ref[SUBLANES_F32 - 1:SUBLANES_F32, :])
    row = lax.broadcasted_iota(jnp.int32, x.shape, 0)
    return x, jnp.where(row == 0, prev, pltpu.roll(x, 1, axis=0)) - x


def _rwkv_rkv_kernel(x_ref, xh_ref, mu_ref, w_ref, o_ref, *, tm, seq):
    x, xx = _shift_delta(x_ref, xh_ref, tm, seq, pl.program_id(1))
    xin = (x + xx * mu_ref[0]).astype(BF16)
    o_ref[0] = _dot(xin, w_ref[0])


def rwkv_rkv(x, mu3, wrkv, seq, tm=512):
    m, d = x.shape
    tm = min(tm, seq)
    assert m % tm == 0 and seq % tm == 0
    r = tm // SUBLANES_F32
    return pl.pallas_call(
        functools.partial(_rwkv_rkv_kernel, tm=tm, seq=seq),
        grid=(3, m // tm),
        in_specs=[pl.BlockSpec((tm, d), lambda g, i: (i, 0)),
                  pl.BlockSpec((SUBLANES_F32, d), lambda g, i: (jnp.maximum(i * r - 1, 0), 0)),
                  pl.BlockSpec((1, 1, d), lambda g, i: (g, 0, 0)),
                  pl.BlockSpec((1, d, d), lambda g, i: (g, 0, 0))],
        out_specs=pl.BlockSpec((1, tm, d), lambda g, i: (g, i, 0)),
        out_shape=jax.ShapeDtypeStruct((3, m, d), F32),
        compiler_params=_cparams(("parallel", "parallel")),
        name="rwkv_rkv",
    )(x, x, mu3, wrkv)


def _rwkv_lora_kernel(x_ref, xh_ref, mu_ref, wl_ref, hid_ref, *, tm, seq):
    x, xx = _shift_delta(x_ref, xh_ref, tm, seq, pl.program_id(0))

    def lora(j):
        xin = (x + xx * mu_ref[j:j + 1, :]).astype(BF16)
        return _dot(xin, wl_ref[:, j * LORA_PAD:(j + 1) * LORA_PAD])

    hid_ref[:, 0 * LORA_PAD:1 * LORA_PAD] = jnp.tanh(lora(0)).astype(BF16)
    hid_ref[:, 1 * LORA_PAD:2 * LORA_PAD] = lora(1).astype(BF16)
    hid_ref[:, 2 * LORA_PAD:3 * LORA_PAD] = lora(2).astype(BF16)
    hid_ref[:, 3 * LORA_PAD:4 * LORA_PAD] = jax.nn.sigmoid(lora(3)).astype(BF16)


def rwkv_lora(x, mu4, wl, seq, tm=512):
    m, d = x.shape
    tm = min(tm, seq)
    assert m % tm == 0 and seq % tm == 0
    r = tm // SUBLANES_F32
    return pl.pallas_call(
        functools.partial(_rwkv_lora_kernel, tm=tm, seq=seq),
        grid=(m // tm,),
        in_specs=[pl.BlockSpec((tm, d), lambda i: (i, 0)),
                  pl.BlockSpec((SUBLANES_F32, d), lambda i: (jnp.maximum(i * r - 1, 0), 0)),
                  pl.BlockSpec((SUBLANES_F32, d), lambda i: (0, 0)),
                  pl.BlockSpec((d, 4 * LORA_PAD), lambda i: (0, 0))],
        out_specs=pl.BlockSpec((tm, 4 * LORA_PAD), lambda i: (i, 0)),
        out_shape=jax.ShapeDtypeStruct((m, 4 * LORA_PAD), BF16),
        compiler_params=_cparams(("parallel",)),
        name="rwkv_lora",
    )(x, x, mu4, wl)


def _softplus(x):
    return jnp.maximum(x, 0.0) + jnp.log(1.0 + jnp.exp(-jnp.abs(x)))


def _wkv_masks(tb):
    lc = WKV_CHUNK
    ri, ci = np.indices((tb, tb))
    same_chunk = (ri // lc) == (ci // lc)
    masks = [same_chunk & (ri > ci), same_chunk & (ri >= ci)]
    for lb in range(lc.bit_length() - 1):
        masks.append(((ri >> (lb + 1)) == (ci >> (lb + 1))) & (((ri >> lb) & 1) == 1) & (((ci >> lb) & 1) == 0))
    return np.stack(masks).astype(np.float32)


def _wkv_kernel(*refs, use_vres, tb, pairs):
    if use_vres:
        (m_ref, r_ref, k_ref, v_ref, vf_ref, hid_ref, w2_ref, a2_ref, v2_ref, g2_ref, vec_ref, z_ref, s_ref) = refs
    else:
        (m_ref, r_ref, k_ref, v_ref, hid_ref, w2_ref, a2_ref, g2_ref, vec_ref, z_ref, s_ref) = refs
    n = RWKV_HEAD
    lc = WKV_CHUNK
    nlev = lc.bit_length() - 1

    @pl.when(pl.program_id(2) == 0)
    def _():
        s_ref[...] = jnp.zeros_like(s_ref)

    lane = lax.broadcasted_iota(jnp.int32, (1, LANES), 1)
    head_masks = (lane < n, lane >= n)
    pr = lax.broadcasted_iota(jnp.int32, (LANES, LANES), 0)
    pc = lax.broadcasted_iota(jnp.int32, (LANES, LANES), 1)
    same_head = jnp.where((pr >= n) == (pc >= n), 1.0, 0.0)
    p_sum = same_head.astype(BF16)
    p_mean = (same_head * (1.0 / n)).astype(BF16)

    def head_reduce(x, p):
        return _dot(x.astype(BF16), p)

    hid = hid_ref[...]
    strict, incl = m_ref[0], m_ref[1]
    tril = incl.astype(BF16)
    lanes = [slice(p * LANES, (p + 1) * LANES) for p in range(pairs)]

    tok = []
    for ln in lanes:
        r, k, v = r_ref[0, :, ln], k_ref[0, :, ln], v_ref[0, :, ln]
        vec = vec_ref[:, ln]
        w0, a0, v0, k_k, k_a, lnx_g, lnx_b, r_k = (vec[j:j + 1, :] for j in range(8))
        wlog = -_softplus(-(w0 + _dot(hid[:, 0 * LORA_PAD:1 * LORA_PAD], w2_ref[:, ln]))) - 0.5
        ld = -jnp.exp(wlog)
        a = jax.nn.sigmoid(a0 + _dot(hid[:, 1 * LORA_PAD:2 * LORA_PAD], a2_ref[:, ln]))
        g = _dot(hid[:, 3 * LORA_PAD:4 * LORA_PAD], g2_ref[:, ln])
        if use_vres:
            v = v + (vf_ref[:, ln] - v) * jax.nn.sigmoid(v0 + _dot(hid[:, 2 * LORA_PAD:3 * LORA_PAD], v2_ref[:, ln]))
        kk = k * k_k
        kk = kk / jnp.maximum(jnp.sqrt(head_reduce(kk * kk, p_sum)), 1e-12)
        k2 = k * (1.0 + (a - 1.0) * k_a)
        hi, lo = _split2(ld)
        cs = _dot(tril, hi) + _dot(tril, lo)
        e_pos = jnp.exp(cs)
        e_neg = jnp.exp(-cs)
        tok.append(dict(rt=r * e_pos, at=-kk * jnp.exp(cs - ld), btb=(kk * a * e_neg).astype(BF16),
                        ktb=(k2 * e_neg).astype(BF16), vb=v.astype(BF16), e_pos=e_pos, g=g, v=v,
                        rk=r * k2 * r_k, lnx_g=lnx_g, lnx_b=lnx_b))

    chains = []
    for t in tok:
        rhs = jnp.concatenate([t["btb"], t["ktb"]], axis=0)
        t["y0"] = jnp.zeros((tb, LANES), F32)
        for hm in head_masks:
            at_h = jnp.where(hm, t["at"], 0.0)
            rt_h = jnp.where(hm, t["rt"], 0.0)
            amat = _dot_nt(jnp.concatenate([at_h, rt_h], axis=0).astype(BF16), rhs)
            a_k = jnp.concatenate([amat[:tb, tb:] * strict, amat[tb:, tb:] * incl], axis=0).astype(BF16)
            qy = _dot(a_k, t["vb"])
            t["y0"] = t["y0"] + jnp.where(hm, qy[tb:], 0.0)
            chains.append(dict(tok=t, hm=hm, a_rb=(amat[tb:, :tb] * incl).astype(BF16),
                               g=amat[:tb, :tb] * strict,
                               rhs=jnp.concatenate([at_h, jnp.where(hm, qy[:tb], 0.0)], axis=1)))
    for lb in range(nlev):
        for ch in chains:
            g = ch["g"]
            ch["g"] = g + _dot((g * m_ref[2 + lb]).astype(BF16), g.astype(BF16))
    for t in tok:
        t["w_uv"] = jnp.zeros((tb, 2 * LANES), F32)
    for ch in chains:
        t = ch["tok"]
        t["w_uv"] = t["w_uv"] + ch["rhs"] + _dot(ch["g"].astype(BF16), ch["rhs"].astype(BF16))
    for t in tok:
        t["w_uvb"] = t["w_uv"].astype(BF16)
        t["rq"], t["y1"] = t["rt"], t["y0"]
    for ch in chains:
        t = ch["tok"]
        x = _dot(ch["a_rb"], t["w_uvb"])
        t["rq"] = t["rq"] + jnp.where(ch["hm"], x[:, :LANES], 0.0)
        t["y1"] = t["y1"] + jnp.where(ch["hm"], x[:, LANES:], 0.0)

    for p, t in enumerate(tok):
        t["rqb"] = t["rq"].astype(BF16)
        t["s"] = s_ref[p]
        t["ys"] = []
    for c in range(tb // lc):
        sl = slice(c * lc, (c + 1) * lc)
        for t in tok:
            d_c = (_dot_tn(t["w_uvb"][sl, :LANES], t["btb"][sl]) * same_head).astype(BF16)
            c_c = _dot_tn(jnp.concatenate([t["w_uvb"][sl, LANES:], t["vb"][sl]], axis=0),
                          jnp.concatenate([t["btb"][sl], t["ktb"][sl]], axis=0)) * same_head
            s = t["s"]
            sb = s.astype(BF16)
            t["ys"].append(_dot_nt(t["rqb"][sl], sb) + t["y1"][sl])
            t["s"] = (s + _dot(sb, d_c) + c_c) * t["e_pos"][(c + 1) * lc - 1:(c + 1) * lc, :]

    for p, (ln, t) in enumerate(zip(lanes, tok)):
        s_ref[p] = t["s"]
        y = jnp.concatenate(t["ys"], axis=0)
        yc = y - head_reduce(y, p_mean)
        yn = yc * lax.rsqrt(head_reduce(yc * yc, p_mean) + GN_EPS)
        bonus = head_reduce(t["rk"], p_sum) * t["v"]
        z_ref[:, ln] = ((yn * t["lnx_g"] + t["lnx_b"] + bonus) * t["g"]).astype(z_ref.dtype)


def wkv(rkv, vfirst, hid, w2, a2, v2, g2, vec, batch, seq, tb=256, pairs=4):
    _, m, d = rkv.shape
    tb = min(tb, seq)
    pairs = min(pairs, d // LANES)
    wl = pairs * LANES
    assert seq % tb == 0 and tb % WKV_CHUNK == 0 and d % wl == 0
    nt = seq // tb
    use_vres = vfirst is not None
    masks = jnp.asarray(_wkv_masks(tb))
    tile = lambda b, p, t: (b * nt + t, p)
    lora_spec = pl.BlockSpec((LORA_PAD, wl), lambda b, p, t: (0, p))
    in_specs = [pl.BlockSpec(masks.shape, lambda b, p, t: (0, 0, 0)),
                pl.BlockSpec((1, tb, wl), lambda b, p, t: (0, b * nt + t, p)),
                pl.BlockSpec((1, tb, wl), lambda b, p, t: (1, b * nt + t, p)),
                pl.BlockSpec((1, tb, wl), lambda b, p, t: (2, b * nt + t, p))]
    args = [masks, rkv, rkv, rkv]
    if use_vres:
        in_specs.append(pl.BlockSpec((tb, wl), tile))
        args.append(vfirst)
    in_specs.append(pl.BlockSpec((tb, 4 * LORA_PAD), lambda b, p, t: (b * nt + t, 0)))
    args.append(hid)
    for wgt in ((w2, a2, v2, g2) if use_vres else (w2, a2, g2)):
        in_specs.append(lora_spec)
        args.append(wgt)
    in_specs.append(pl.BlockSpec((SUBLANES_F32, wl), lambda b, p, t: (0, p)))
    args.append(vec)
    return pl.pallas_call(
        functools.partial(_wkv_kernel, use_vres=use_vres, tb=tb, pairs=pairs),
        grid=(batch, d // wl, nt),
        in_specs=in_specs,
        out_specs=pl.BlockSpec((tb, wl), tile),
        out_shape=jax.ShapeDtypeStruct((m, d), BF16),
        scratch_shapes=[pltpu.VMEM((pairs, LANES, LANES), F32)],
        compiler_params=_cparams(("parallel", "parallel", "arbitrary")),
        name="wkv",
    )(*args)


def _pad_to(x, axis, size):
    pad = [(0, 0)] * x.ndim
    pad[axis] = (0, size - x.shape[axis])
    return jnp.pad(x, pad)


def _rwkv_layer(x, xshape, j, p, v_first):
    batch, seq, d = xshape
    mu = p["rw_mu"][j]
    mu3 = jnp.stack([mu[0], mu[2], mu[3]])[:, None, :]
    mu4 = _pad_to(jnp.stack([mu[1], mu[4], mu[3], mu[5]]), 0, SUBLANES_F32)
    use_vres = j > 0
    v1 = p["rw_v1"][j - 1] if use_vres else jnp.zeros((d, LORA_PAD), F32)
    wl = jnp.concatenate([_pad_to(w, 1, LORA_PAD) for w in (p["rw_w1"][j], p["rw_a1"][j], v1, p["rw_g1"][j])],
                         axis=1).astype(BF16)
    up = lambda w: _pad_to(w, 0, LORA_PAD).astype(BF16)
    vec = jnp.stack([p["rw_w0"][j], p["rw_a0"][j], p["rw_v0"][j - 1] if use_vres else jnp.zeros((d,), F32),
                     p["rw_k_k"][j], p["rw_k_a"][j], p["rw_lnx_g"][j], p["rw_lnx_b"][j], p["rw_r_k"][j].reshape(d)])
    rkv = rwkv_rkv(x, mu3, p["rw_w_rkv"][j].astype(BF16), seq)
    hid = rwkv_lora(x, mu4, wl, seq)
    if not use_vres:
        v_first = rkv[2]
    z = wkv(rkv, v_first if use_vres else None, hid, up(p["rw_w2"][j]), up(p["rw_a2"][j]),
            up(p["rw_v2"][j - 1]) if use_vres else None, up(p["rw_g2"][j]), vec, batch, seq)
    return z, p["rw_w_o"][j].astype(BF16), v_first


def _attn_layer(xb, xshape, i, j, p, btiles, tq):
    batch, seq, d = xshape
    lambda_init = 0.8 - 0.6 * math.exp(-0.3 * i)
    qkv = matmul(xb, p["da_w_qkv"][j].astype(BF16), BF16)
    o = diff_attn(qkv, btiles, p["rel_bias"], p["da_lam"][j], p["da_subln_g"][j], batch, seq, lambda_init, tq=tq)
    return o, p["da_w_o"][j].astype(BF16)


def kernel(x, ln_g, ln_b, rw_mu, rw_w_rkv, rw_w0, rw_w1, rw_w2, rw_a0, rw_a1, rw_a2, rw_v0, rw_v1, rw_v2, rw_g1, rw_g2, rw_k_k, rw_k_a, rw_r_k, rw_lnx_g, rw_lnx_b, rw_w_o, da_w_qkv, da_lam, da_subln_g, da_w_o, rel_bias, ff_w_up, ff_conv_w, ff_conv_b, ff_w_down):
    p = dict(rw_mu=rw_mu, rw_w_rkv=rw_w_rkv, rw_w0=rw_w0, rw_w1=rw_w1, rw_w2=rw_w2, rw_a0=rw_a0, rw_a1=rw_a1,
             rw_a2=rw_a2, rw_v0=rw_v0, rw_v1=rw_v1, rw_v2=rw_v2, rw_g1=rw_g1, rw_g2=rw_g2, rw_k_k=rw_k_k,
             rw_k_a=rw_k_a, rw_r_k=rw_r_k, rw_lnx_g=rw_lnx_g, rw_lnx_b=rw_lnx_b, rw_w_o=rw_w_o,
             da_w_qkv=da_w_qkv, da_lam=da_lam, da_subln_g=da_subln_g, da_w_o=da_w_o, rel_bias=rel_bias)
    batch, seq, d = x.shape
    xshape = (batch, seq, d)
    f = ff_conv_b.shape[1]
    fp = -(-f // FFN_TILE) * FFN_TILE
    tq = min(ATTN_TILE, seq)
    btiles = bias_tiles(rel_bias, tq, tq)
    xf = x.reshape(batch * seq, d)
    xb = xf.astype(BF16)
    v_first = None
    for i in range(DEPTH):
        j = i // 2
        if i % 2 == 0:
            z, w_o, v_first = _rwkv_layer(xf, xshape, j, p, v_first)
        else:
            z, w_o = _attn_layer(xb, xshape, i, j, p, btiles, tq)
        xf, xb = proj_ln(z, w_o, xf, ln_g[i, 0], ln_b[i, 0])
        wu = _pad_to(ff_w_up[i][:, :f].astype(BF16), 1, fp)
        wg = _pad_to(ff_w_up[i][:, f:].astype(BF16), 1, fp)
        cw = _pad_to(_pad_to(ff_conv_w[i], 1, fp), 0, SUBLANES_F32)
        cb = _pad_to(ff_conv_b[i][None, :], 1, fp)
        h = ffn_up(xb, wu, wg, cw, cb, seq, tn=FFN_TILE)
        wd = _pad_to(ff_w_down[i].astype(BF16), 0, fp)
        xf, xb = proj_ln(h, wd, xf, ln_g[i, 1], ln_b[i, 1], tk=fp // FFN_DOWN_KSTEPS)
    return xf.reshape(batch, seq, d)
```

```python
import functools
import math

import numpy as np
import jax
import jax.numpy as jnp
from jax import lax
from jax.experimental import pallas as pl
from jax.experimental.pallas import tpu as pltpu

F32 = jnp.float32
BF16 = jnp.bfloat16

DEPTH = 4
RWKV_HEAD = 64
GN_EPS = 64e-5
DIFF_QK = 128
DIFF_V = 256
REL_BUCKETS = 32
REL_MAX_EXACT = 16
REL_MAX_DIST = 128
CONV_W = 3
LN_EPS = 1e-5
ALPHA = (2 * DEPTH) ** 0.25
NEG_INF = -1e30
LOG2E = math.log2(math.e)

LANES = 128
SUBLANES_F32 = 8
SUBLANES_BF16 = 16
VMEM_LIMIT_BYTES = 60000 * 1024

WKV_CHUNK = 64
LORA_PAD = 128
ATTN_TILE = 512
FFN_TILE = 512
FFN_SUBTILE = 256
FFN_SUBROWS = 512
FFN_DOWN_KSTEPS = 4
PROJ_SUBROWS = 256


def _cparams(sem):
    return pltpu.CompilerParams(dimension_semantics=sem, vmem_limit_bytes=VMEM_LIMIT_BYTES)


def _dot(a, b):
    return jnp.dot(a, b, preferred_element_type=F32)


def _dot_nt(a, b):
    return lax.dot_general(a, b, (((1,), (1,)), ((), ())), preferred_element_type=F32)


def _dot_tn(a, b):
    return lax.dot_general(a, b, (((0,), (0,)), ((), ())), preferred_element_type=F32)


def _split2(x):
    hi = x.astype(BF16)
    lo = (x - hi.astype(F32)).astype(BF16)
    return hi, lo


def _split3(x):
    hi = x.astype(BF16)
    r1 = x - hi.astype(F32)
    mid = r1.astype(BF16)
    lo = (r1 - mid.astype(F32)).astype(BF16)
    return hi, mid, lo


def _layer_norm(y, g, b):
    mu = jnp.mean(y, axis=-1, keepdims=True)
    yc = y - mu
    var = jnp.mean(yc * yc, axis=-1, keepdims=True)
    return yc * lax.rsqrt(var + LN_EPS) * g + b


def _mm_kernel(a_ref, w_ref, o_ref):
    o_ref[...] = _dot(a_ref[...], w_ref[...]).astype(o_ref.dtype)


def matmul(a, w, out_dtype, tm=1024, tn=1024):
    m, k = a.shape
    n = w.shape[1]
    tm, tn = min(tm, m), min(tn, n)
    assert m % tm == 0 and n % tn == 0
    return pl.pallas_call(
        _mm_kernel,
        grid=(m // tm, n // tn),
        in_specs=[pl.BlockSpec((tm, k), lambda i, j: (i, 0)),
                  pl.BlockSpec((k, tn), lambda i, j: (0, j))],
        out_specs=pl.BlockSpec((tm, tn), lambda i, j: (i, j)),
        out_shape=jax.ShapeDtypeStruct((m, n), out_dtype),
        compiler_params=_cparams(("parallel", "parallel")),
        name="matmul",
    )(a, w)


def _proj_ln_kernel(a_ref, w_ref, res_ref, g_ref, b_ref, xo_ref, xb_ref, *scratch, nk):
    tm = a_ref.shape[0]
    tr = min(PROJ_SUBROWS, tm)
    row_groups = [slice(r0, r0 + tr) for r0 in range(0, tm, tr)]

    def finish(partial):
        accs = [partial(rs) for rs in row_groups]
        for rs, acc in zip(row_groups, accs):
            o = _layer_norm(ALPHA * res_ref[rs, :] + acc, g_ref[...], b_ref[...])
            xo_ref[rs, :] = o
            xb_ref[rs, :] = o.astype(BF16)

    if nk == 1:
        finish(lambda rs: _dot(a_ref[rs, :], w_ref[...]))
        return
    acc_ref, = scratch
    k = pl.program_id(1)

    @pl.when(k == 0)
    def _():
        acc_ref[...] = _dot(a_ref[...], w_ref[...])

    @pl.when((k > 0) & (k < nk - 1))
    def _():
        acc_ref[...] += _dot(a_ref[...], w_ref[...])

    @pl.when(k == nk - 1)
    def _():
        finish(lambda rs: acc_ref[rs, :] + _dot(a_ref[rs, :], w_ref[...]))


def proj_ln(a, w, res, g, b, tm=512, tk=None):
    m, kdim = a.shape
    d = w.shape[1]
    tm = min(tm, m)
    tk = kdim if tk is None else tk
    assert m % tm == 0 and kdim % tk == 0
    nk = kdim // tk
    return pl.pallas_call(
        functools.partial(_proj_ln_kernel, nk=nk),
        grid=(m // tm, nk),
        in_specs=[pl.BlockSpec((tm, tk), lambda i, k: (i, k)),
                  pl.BlockSpec((tk, d), lambda i, k: (k, 0)),
                  pl.BlockSpec((tm, d), lambda i, k: (i, 0)),
                  pl.BlockSpec((1, d), lambda i, k: (0, 0)),
                  pl.BlockSpec((1, d), lambda i, k: (0, 0))],
        out_specs=[pl.BlockSpec((tm, d), lambda i, k: (i, 0)),
                   pl.BlockSpec((tm, d), lambda i, k: (i, 0))],
        out_shape=[jax.ShapeDtypeStruct((m, d), F32), jax.ShapeDtypeStruct((m, d), BF16)],
        scratch_shapes=[pltpu.VMEM((tm, d), F32)] if nk > 1 else [],
        compiler_params=_cparams(("parallel", "arbitrary")),
        name="proj_ln",
    )(a, w, res, g.reshape(1, d), b.reshape(1, d))


def _ffn_up_kernel(x_ref, xh_ref, wu_ref, wg_ref, cw_ref, cb_ref, h_ref, *, tm, seq):
    i = pl.program_id(0)
    x = x_ref[...]
    first = (i * tm) % seq == 0
    halo = jnp.where(first, jnp.zeros_like(xh_ref[...]), xh_ref[...])
    xe = jnp.concatenate([halo, x], axis=0)
    hl = halo.shape[0]
    tn = h_ref.shape[1]
    tr = min(FFN_SUBROWS, tm)
    subs = [(slice(r0, r0 + tr), slice(c0, min(c0 + FFN_SUBTILE, tn)))
            for r0 in range(0, tm, tr) for c0 in range(0, tn, FFN_SUBTILE)]
    ug = [(_dot(x[rs], wu_ref[:, cs]), _dot(xe[rs.start:rs.stop + hl], wg_ref[:, cs])) for rs, cs in subs]
    for (rs, cs), (u, ge) in zip(subs, ug):
        g0 = ge[hl:, :]
        g1 = pltpu.roll(ge, 1, axis=0)[hl:, :]
        g2 = pltpu.roll(ge, 2, axis=0)[hl:, :]
        gc = cb_ref[:, cs] + g2 * cw_ref[0:1, cs] + g1 * cw_ref[1:2, cs] + g0 * cw_ref[2:3, cs]
        h_ref[rs, cs] = (u * jax.nn.gelu(gc)).astype(h_ref.dtype)


def ffn_up(xb, wu, wg, conv_w, conv_b, seq, tm=1024, tn=512):
    m, d = xb.shape
    fp = wu.shape[1]
    tm, tn = min(tm, seq), min(tn, fp)
    hl = SUBLANES_BF16
    assert m % tm == 0 and fp % tn == 0 and seq % tm == 0 and tm % hl == 0
    r = tm // hl
    return pl.pallas_call(
        functools.partial(_ffn_up_kernel, tm=tm, seq=seq),
        grid=(m // tm, fp // tn),
        in_specs=[pl.BlockSpec((tm, d), lambda i, j: (i, 0)),
                  pl.BlockSpec((hl, d), lambda i, j: (jnp.maximum(i * r - 1, 0), 0)),
                  pl.BlockSpec((d, tn), lambda i, j: (0, j)),
                  pl.BlockSpec((d, tn), lambda i, j: (0, j)),
                  pl.BlockSpec((SUBLANES_F32, tn), lambda i, j: (0, j)),
                  pl.BlockSpec((1, tn), lambda i, j: (0, j))],
        out_specs=pl.BlockSpec((tm, tn), lambda i, j: (i, j)),
        out_shape=jax.ShapeDtypeStruct((m, fp), BF16),
        compiler_params=_cparams(("parallel", "parallel")),
        name="ffn_up",
    )(xb, xb, wu, wg, conv_w, conv_b)


def _bucket_thresholds():
    n = np.arange(1, 4 * REL_MAX_DIST, dtype=np.float64)
    large = REL_MAX_EXACT + np.floor(np.log(n / REL_MAX_EXACT) / math.log(REL_MAX_DIST / REL_MAX_EXACT)
                                     * (REL_BUCKETS - REL_MAX_EXACT)).astype(np.int64)
    bucket = np.where(n < REL_MAX_EXACT, n.astype(np.int64), np.minimum(large, REL_BUCKETS - 1))
    thr = [int(n[i]) for i in range(len(n)) if i == 0 or bucket[i] != bucket[i - 1]]
    assert bucket[0] == 1 and len(thr) == REL_BUCKETS - 1 and np.all(np.diff(bucket) >= 0)
    return thr


_BUCKET_THR = _bucket_thresholds()
REL_FAR = _BUCKET_THR[-1]


def _bias_tile_kernel(rb_ref, o_ref, *, tq, tk):
    h = pl.program_id(0)
    w = pl.program_id(1)
    row = lax.broadcasted_iota(jnp.int32, (tq, tk), 0)
    col = lax.broadcasted_iota(jnp.int32, (tq, tk), 1)
    dist = w * tk + row - col
    val = jnp.full((tq, tk), rb_ref[0, h], F32)
    for m, t in enumerate(_BUCKET_THR):
        val = jnp.where(dist >= t, rb_ref[m + 1, h], val)
    o_ref[0, 0] = jnp.where(dist >= 0, val * LOG2E, NEG_INF)


def bias_tiles(rel_bias, tq, tk):
    nh = rel_bias.shape[1]
    return pl.pallas_call(
        functools.partial(_bias_tile_kernel, tq=tq, tk=tk),
        grid=(nh, 2),
        in_specs=[pl.BlockSpec(memory_space=pltpu.SMEM)],
        out_specs=pl.BlockSpec((1, 1, tq, tk), lambda h, w: (h, w, 0, 0)),
        out_shape=jax.ShapeDtypeStruct((nh, 2, tq, tk), F32),
        compiler_params=_cparams(("parallel", "parallel")),
        name="bias_tiles",
    )(rel_bias)


def _diff_attn_kernel(rb_ref, q_ref, k_ref, v_ref, bias_ref, lam_ref, sg_ref, o_ref,
                      acc_ref, s_ref, p_ref, *, tq, tk, lambda_init):
    h = pl.program_id(1)
    qi = pl.program_id(2)
    q = q_ref[...].astype(F32) * (DIFF_QK ** -0.5 * LOG2E)
    qs = (q[:, :DIFF_QK].astype(BF16), q[:, DIFF_QK:].astype(BF16))
    acc_ref[...] = jnp.zeros_like(acc_ref)
    far_bias = rb_ref[REL_BUCKETS - 1, h] * LOG2E

    def scores(c, first_blk, nb, buf):
        start = pl.multiple_of(first_blk * tk, tk)
        s_ref[buf, c, :, :nb * tk] = _dot_nt(qs[c], k_ref[pl.ds(start, nb * tk), c * DIFF_QK:(c + 1) * DIFF_QK])

    def softmax_pv(c, first_blk, nb, buf, carry, near):
        w = nb * tk
        start = pl.multiple_of(first_blk * tk, tk)
        m, l = carry[2 * c], carry[2 * c + 1]
        s = s_ref[buf, c, :, :w]
        if near:
            s = s + (bias_ref[0, 0] if nb == 1 else jnp.concatenate([bias_ref[0, 1], bias_ref[0, 0]], axis=1))
            m_new = jnp.maximum(m, jnp.max(s, axis=-1, keepdims=True))
            shift = m_new
        else:
            m_new = jnp.maximum(m, jnp.max(s, axis=-1, keepdims=True) + far_bias)
            shift = m_new - far_bias
        alpha = jnp.exp2(m - m_new)
        p = jnp.exp2(s - shift)
        l_new = alpha * l + jnp.sum(p, axis=-1, keepdims=True)
        p_ref[c, :, :w] = p.astype(BF16)
        acc_ref[c] = alpha * acc_ref[c] + _dot(p_ref[c, :, :w], v_ref[pl.ds(start, w), :])
        return carry[:2 * c] + (m_new, l_new) + carry[2 * c + 2:]

    def unpipelined(first_blk, nb, carry, near):
        for c in range(2):
            scores(c, first_blk, nb, 0)
        for c in range(2):
            carry = softmax_pv(c, first_blk, nb, 0, carry, near)
        return carry

    odd = (qi + 1) & 1
    n_pairs = (qi + 1) >> 1
    carry = (jnp.full((tq, 1), NEG_INF, F32), jnp.zeros((tq, 1), F32),
             jnp.full((tq, 1), NEG_INF, F32), jnp.zeros((tq, 1), F32))
    carry = lax.cond(odd == 1,
                     lambda c: lax.cond(qi == 0, lambda d: unpipelined(0, 1, d, True),
                                        lambda d: unpipelined(0, 1, d, False), c),
                     lambda c: c, carry)

    def far_pair(blk, buf, carry):
        scores(0, blk + 2, 2, 1 - buf)
        carry = softmax_pv(0, blk, 2, buf, carry, False)
        scores(1, blk + 2, 2, 1 - buf)
        return softmax_pv(1, blk, 2, buf, carry, False)

    def near_pair(buf, carry):
        for c in range(2):
            carry = softmax_pv(c, qi - 1, 2, buf, carry, True)
        return carry

    def pairs(carry):
        for c in range(2):
            scores(c, odd, 2, 0)
        n_far = n_pairs - 1
        carry = lax.fori_loop(0, n_far >> 1,
                              lambda i, c: far_pair(odd + 4 * i + 2, 1, far_pair(odd + 4 * i, 0, c)), carry)
        return lax.cond((n_far & 1) == 1,
                        lambda c: near_pair(1, far_pair(odd + 2 * (n_far - 1), 0, c)),
                        lambda c: near_pair(0, c), carry)

    m0, l0, m1, l1 = lax.cond(n_pairs >= 1, pairs, lambda c: c, carry)

    lam = lam_ref[...]
    lam_full = (jnp.exp(jnp.sum(lam[0:1, :] * lam[1:2, :], axis=-1, keepdims=True))
                - jnp.exp(jnp.sum(lam[2:3, :] * lam[3:4, :], axis=-1, keepdims=True)) + lambda_init)
    o = acc_ref[0] / l0 - lam_full * (acc_ref[1] / l1)
    o = o * lax.rsqrt(jnp.mean(o * o, axis=-1, keepdims=True) + LN_EPS) * sg_ref[...] * (1.0 - lambda_init)
    o_ref[...] = o.astype(o_ref.dtype)


def diff_attn(qkv, btiles, rel_bias, lam, subln_g, batch, seq, lambda_init, tq=512):
    m, d3 = qkv.shape
    d = d3 // 3
    nh = d // DIFF_V
    tq = min(tq, seq)
    tk = tq
    assert seq % tq == 0 and tk >= REL_FAR and btiles.shape == (nh, 2, tq, tk)
    nq = seq // tq
    return pl.pallas_call(
        functools.partial(_diff_attn_kernel, tq=tq, tk=tk, lambda_init=lambda_init),
        grid=(batch, nh, nq),
        in_specs=[pl.BlockSpec(memory_space=pltpu.SMEM),
                  pl.BlockSpec((tq, DIFF_V), lambda b, h, i: (b * nq + i, h)),
                  pl.BlockSpec((seq, DIFF_V), lambda b, h, i: (b, nh + h)),
                  pl.BlockSpec((seq, DIFF_V), lambda b, h, i: (b, 2 * nh + h)),
                  pl.BlockSpec((1, 2, tq, tk), lambda b, h, i: (h, 0, 0, 0)),
                  pl.BlockSpec((4, DIFF_QK), lambda b, h, i: (0, 0)),
                  pl.BlockSpec((1, DIFF_V), lambda b, h, i: (0, 0))],
        out_specs=pl.BlockSpec((tq, DIFF_V), lambda b, h, i: (b * nq + i, h)),
        out_shape=jax.ShapeDtypeStruct((m, d), BF16),
        scratch_shapes=[pltpu.VMEM((2, tq, DIFF_V), F32),
                        pltpu.VMEM((2, 2, tq, 2 * tk), F32),
                        pltpu.VMEM((2, tq, 2 * tk), BF16)],
        compiler_params=_cparams(("parallel", "parallel", "arbitrary")),
        name="diff_attn",
    )(rel_bias, qkv, qkv, qkv, btiles, lam, subln_g.reshape(1, DIFF_V))


def _shift_delta(x_ref, xh_ref, tm, seq, i):
    x = x_ref[...]
    first = (i * tm) % seq == 0
    prev = jnp.where(first, jnp.zeros((1, x.shape[1]), F32), xh_ref[SUBLANES_F32 - 1:SUBLANES_F32, :])
    row = lax.broadcasted_iota(jnp.int32, x.shape, 0)
    return x, jnp.where(row == 0, prev, pltpu.roll(x, 1, axis=0)) - x


def _rwkv_rkv_kernel(x_ref, xh_ref, mu_ref, w_ref, o_ref, *, tm, seq):
    x, xx = _shift_delta(x_ref, xh_ref, tm, seq, pl.program_id(1))
    xin = (x + xx * mu_ref[0]).astype(BF16)
    o_ref[0] = _dot(xin, w_ref[0])


def rwkv_rkv(x, mu3, wrkv, seq, tm=512):
    m, d = x.shape
    tm = min(tm, seq)
    assert m % tm == 0 and seq % tm == 0
    r = tm // SUBLANES_F32
    return pl.pallas_call(
        functools.partial(_rwkv_rkv_kernel, tm=tm, seq=seq),
        grid=(3, m // tm),
        in_specs=[pl.BlockSpec((tm, d), lambda g, i: (i, 0)),
                  pl.BlockSpec((SUBLANES_F32, d), lambda g, i: (jnp.maximum(i * r - 1, 0), 0)),
                  pl.BlockSpec((1, 1, d), lambda g, i: (g, 0, 0)),
                  pl.BlockSpec((1, d, d), lambda g, i: (g, 0, 0))],
        out_specs=pl.BlockSpec((1, tm, d), lambda g, i: (g, i, 0)),
        out_shape=jax.ShapeDtypeStruct((3, m, d), F32),
        compiler_params=_cparams(("parallel", "parallel")),
        name="rwkv_rkv",
    )(x, x, mu3, wrkv)


def _rwkv_lora_kernel(x_ref, xh_ref, mu_ref, wl_ref, hid_ref, *, tm, seq):
    x, xx = _shift_delta(x_ref, xh_ref, tm, seq, pl.program_id(0))

    def lora(j):
        xin = (x + xx * mu_ref[j:j + 1, :]).astype(BF16)
        return _dot(xin, wl_ref[:, j * LORA_PAD:(j + 1) * LORA_PAD])

    hid_ref[:, 0 * LORA_PAD:1 * LORA_PAD] = jnp.tanh(lora(0)).astype(BF16)
    hid_ref[:, 1 * LORA_PAD:2 * LORA_PAD] = lora(1).astype(BF16)
    hid_ref[:, 2 * LORA_PAD:3 * LORA_PAD] = lora(2).astype(BF16)
    hid_ref[:, 3 * LORA_PAD:4 * LORA_PAD] = jax.nn.sigmoid(lora(3)).astype(BF16)


def rwkv_lora(x, mu4, wl, seq, tm=512):
    m, d = x.shape
    tm = min(tm, seq)
    assert m % tm == 0 and seq % tm == 0
    r = tm // SUBLANES_F32
    return pl.pallas_call(
        functools.partial(_rwkv_lora_kernel, tm=tm, seq=seq),
        grid=(m // tm,),
        in_specs=[pl.BlockSpec((tm, d), lambda i: (i, 0)),
                  pl.BlockSpec((SUBLANES_F32, d), lambda i: (jnp.maximum(i * r - 1, 0), 0)),
                  pl.BlockSpec((SUBLANES_F32, d), lambda i: (0, 0)),
                  pl.BlockSpec((d, 4 * LORA_PAD), lambda i: (0, 0))],
        out_specs=pl.BlockSpec((tm, 4 * LORA_PAD), lambda i: (i, 0)),
        out_shape=jax.ShapeDtypeStruct((m, 4 * LORA_PAD), BF16),
        compiler_params=_cparams(("parallel",)),
        name="rwkv_lora",
    )(x, x, mu4, wl)


def _softplus(x):
    return jnp.maximum(x, 0.0) + jnp.log(1.0 + jnp.exp(-jnp.abs(x)))


def _wkv_masks(tb):
    lc = WKV_CHUNK
    ri, ci = np.indices((tb, tb))
    same_chunk = (ri // lc) == (ci // lc)
    masks = [same_chunk & (ri > ci), same_chunk & (ri >= ci)]
    for lb in range(lc.bit_length() - 1):
        masks.append(((ri >> (lb + 1)) == (ci >> (lb + 1))) & (((ri >> lb) & 1) == 1) & (((ci >> lb) & 1) == 0))
    return np.stack(masks).astype(np.float32)


def _wkv_kernel(*refs, use_vres, tb, pairs):
    if use_vres:
        (m_ref, r_ref, k_ref, v_ref, vf_ref, hid_ref, w2_ref, a2_ref, v2_ref, g2_ref, vec_ref, z_ref, s_ref) = refs
    else:
        (m_ref, r_ref, k_ref, v_ref, hid_ref, w2_ref, a2_ref, g2_ref, vec_ref, z_ref, s_ref) = refs
    n = RWKV_HEAD
    lc = WKV_CHUNK
    nlev = lc.bit_length() - 1

    @pl.when(pl.program_id(2) == 0)
    def _():
        s_ref[...] = jnp.zeros_like(s_ref)

    lane = lax.broadcasted_iota(jnp.int32, (1, LANES), 1)
    head_masks = (lane < n, lane >= n)
    pr = lax.broadcasted_iota(jnp.int32, (LANES, LANES), 0)
    pc = lax.broadcasted_iota(jnp.int32, (LANES, LANES), 1)
    same_head = jnp.where((pr >= n) == (pc >= n), 1.0, 0.0)
    p_sum = same_head.astype(BF16)
    p_mean = (same_head * (1.0 / n)).astype(BF16)

    def head_reduce(x, p):
        return _dot(x.astype(BF16), p)

    hid = hid_ref[...]
    strict, incl = m_ref[0], m_ref[1]
    tril = incl.astype(BF16)
    lanes = [slice(p * LANES, (p + 1) * LANES) for p in range(pairs)]

    tok = []
    for ln in lanes:
        r, k, v = r_ref[0, :, ln], k_ref[0, :, ln], v_ref[0, :, ln]
        vec = vec_ref[:, ln]
        w0, a0, v0, k_k, k_a, lnx_g, lnx_b, r_k = (vec[j:j + 1, :] for j in range(8))
        wlog = -_softplus(-(w0 + _dot(hid[:, 0 * LORA_PAD:1 * LORA_PAD], w2_ref[:, ln]))) - 0.5
        ld = -jnp.exp(wlog)
        a = jax.nn.sigmoid(a0 + _dot(hid[:, 1 * LORA_PAD:2 * LORA_PAD], a2_ref[:, ln]))
        g = _dot(hid[:, 3 * LORA_PAD:4 * LORA_PAD], g2_ref[:, ln])
        if use_vres:
            v = v + (vf_ref[0, :, ln] - v) * jax.nn.sigmoid(v0 + _dot(hid[:, 2 * LORA_PAD:3 * LORA_PAD], v2_ref[:, ln]))
        kk = k * k_k
        kk = kk / jnp.maximum(jnp.sqrt(head_reduce(kk * kk, p_sum)), 1e-12)
        k2 = k * (1.0 + (a - 1.0) * k_a)
        hi, lo = _split2(ld)
        cs = _dot(tril, hi) + _dot(tril, lo)
        e_pos = jnp.exp(cs)
        e_neg = jnp.exp(-cs)
        tok.append(dict(rt=r * e_pos, at=-kk * jnp.exp(cs - ld), btb=(kk * a * e_neg).astype(BF16),
                        ktb=(k2 * e_neg).astype(BF16), vb=v.astype(BF16), e_pos=e_pos, g=g, v=v,
                        rk=r * k2 * r_k, lnx_g=lnx_g, lnx_b=lnx_b))

    chains = []
    for t in tok:
        rhs = jnp.concatenate([t["btb"], t["ktb"]], axis=0)
        t["y0"] = jnp.zeros((tb, LANES), F32)
        for hm in head_masks:
            at_h = jnp.where(hm, t["at"], 0.0)
            rt_h = jnp.where(hm, t["rt"], 0.0)
            amat = _dot_nt(jnp.concatenate([at_h, rt_h], axis=0).astype(BF16), rhs)
            a_k = jnp.concatenate([amat[:tb, tb:] * strict, amat[tb:, tb:] * incl], axis=0).astype(BF16)
            qy = _dot(a_k, t["vb"])
            t["y0"] = t["y0"] + jnp.where(hm, qy[tb:], 0.0)
            chains.append(dict(tok=t, hm=hm, a_rb=(amat[tb:, :tb] * incl).astype(BF16),
                               g=amat[:tb, :tb] * strict,
                               rhs=jnp.concatenate([at_h, jnp.where(hm, qy[:tb], 0.0)], axis=1)))
    for lb in range(nlev):
        for ch in chains:
            g = ch["g"]
            ch["g"] = g + _dot((g * m_ref[2 + lb]).astype(BF16), g.astype(BF16))
    for t in tok:
        t["w_uv"] = jnp.zeros((tb, 2 * LANES), F32)
    for ch in chains:
        t = ch["tok"]
        t["w_uv"] = t["w_uv"] + ch["rhs"] + _dot(ch["g"].astype(BF16), ch["rhs"].astype(BF16))
    for t in tok:
        t["w_uvb"] = t["w_uv"].astype(BF16)
        t["rq"], t["y1"] = t["rt"], t["y0"]
    for ch in chains:
        t = ch["tok"]
        x = _dot(ch["a_rb"], t["w_uvb"])
        t["rq"] = t["rq"] + jnp.where(ch["hm"], x[:, :LANES], 0.0)
        t["y1"] = t["y1"] + jnp.where(ch["hm"], x[:, LANES:], 0.0)

    for p, t in enumerate(tok):
        t["rqb"] = t["rq"].astype(BF16)
        t["s"] = s_ref[p]
        t["ys"] = []
    for c in range(tb // lc):
        sl = slice(c * lc, (c + 1) * lc)
        for t in tok:
            d_c = (_dot_tn(t["w_uvb"][sl, :LANES], t["btb"][sl]) * same_head).astype(BF16)
            c_c = _dot_tn(jnp.concatenate([t["w_uvb"][sl, LANES:], t["vb"][sl]], axis=0),
                          jnp.concatenate([t["btb"][sl], t["ktb"][sl]], axis=0)) * same_head
            s = t["s"]
            sb = s.astype(BF16)
            t["ys"].append(_dot_nt(t["rqb"][sl], sb) + t["y1"][sl])
            t["s"] = (s + _dot(sb, d_c) + c_c) * t["e_pos"][(c + 1) * lc - 1:(c + 1) * lc, :]

    for p, (ln, t) in enumerate(zip(lanes, tok)):
        s_ref[p] = t["s"]
        y = jnp.concatenate(t["ys"], axis=0)
        yc = y - head_reduce(y, p_mean)
        yn = yc * lax.rsqrt(head_reduce(yc * yc, p_mean) + GN_EPS)
        bonus = head_reduce(t["rk"], p_sum) * t["v"]
        z_ref[:, ln] = ((yn * t["lnx_g"] + t["lnx_b"] + bonus) * t["g"]).astype(z_ref.dtype)


def wkv(rkv, vfirst, hid, w2, a2, v2, g2, vec, batch, seq, tb=256, pairs=4):
    _, m, d = rkv.shape
    tb = min(tb, seq)
    pairs = min(pairs, d // LANES)
    wl = pairs * LANES
    assert seq % tb == 0 and tb % WKV_CHUNK == 0 and d % wl == 0
    nt = seq // tb
    use_vres = vfirst is not None
    masks = jnp.asarray(_wkv_masks(tb))
    tile = lambda b, p, t: (b * nt + t, p)
    lora_spec = pl.BlockSpec((LORA_PAD, wl), lambda b, p, t: (0, p))
    in_specs = [pl.BlockSpec(masks.shape, lambda b, p, t: (0, 0, 0)),
                pl.BlockSpec((1, tb, wl), lambda b, p, t: (0, b * nt + t, p)),
                pl.BlockSpec((1, tb, wl), lambda b, p, t: (1, b * nt + t, p)),
                pl.BlockSpec((1, tb, wl), lambda b, p, t: (2, b * nt + t, p))]
    args = [masks, rkv, rkv, rkv]
    if use_vres:
        in_specs.append(pl.BlockSpec((1, tb, wl), lambda b, p, t: (2, b * nt + t, p)))
        args.append(vfirst)
    in_specs.append(pl.BlockSpec((tb, 4 * LORA_PAD), lambda b, p, t: (b * nt + t, 0)))
    args.append(hid)
    for wgt in ((w2, a2, v2, g2) if use_vres else (w2, a2, g2)):
        in_specs.append(lora_spec)
        args.append(wgt)
    in_specs.append(pl.BlockSpec((SUBLANES_F32, wl), lambda b, p, t: (0, p)))
    args.append(vec)
    return pl.pallas_call(
        functools.partial(_wkv_kernel, use_vres=use_vres, tb=tb, pairs=pairs),
        grid=(batch, d // wl, nt),
        in_specs=in_specs,
        out_specs=pl.BlockSpec((tb, wl), tile),
        out_shape=jax.ShapeDtypeStruct((m, d), BF16),
        scratch_shapes=[pltpu.VMEM((pairs, LANES, LANES), F32)],
        compiler_params=_cparams(("parallel", "parallel", "arbitrary")),
        name="wkv",
    )(*args)


def _pad_to(x, axis, size):
    pad = [(0, 0)] * x.ndim
    pad[axis] = (0, size - x.shape[axis])
    return jnp.pad(x, pad)


def _rwkv_layer(x, xshape, j, p, v_first):
    batch, seq, d = xshape
    mu = p["rw_mu"][j]
    mu3 = jnp.stack([mu[0], mu[2], mu[3]])[:, None, :]
    mu4 = _pad_to(jnp.stack([mu[1], mu[4], mu[3], mu[5]]), 0, SUBLANES_F32)
    use_vres = j > 0
    v1 = p["rw_v1"][j - 1] if use_vres else jnp.zeros((d, LORA_PAD), F32)
    wl = jnp.concatenate([_pad_to(w, 1, LORA_PAD) for w in (p["rw_w1"][j], p["rw_a1"][j], v1, p["rw_g1"][j])],
                         axis=1).astype(BF16)
    up = lambda w: _pad_to(w, 0, LORA_PAD).astype(BF16)
    vec = jnp.stack([p["rw_w0"][j], p["rw_a0"][j], p["rw_v0"][j - 1] if use_vres else jnp.zeros((d,), F32),
                     p["rw_k_k"][j], p["rw_k_a"][j], p["rw_lnx_g"][j], p["rw_lnx_b"][j], p["rw_r_k"][j].reshape(d)])
    rkv = rwkv_rkv(x, mu3, p["rw_w_rkv"][j].astype(BF16), seq)
    hid = rwkv_lora(x, mu4, wl, seq)
    if not use_vres:
        v_first = rkv
    z = wkv(rkv, v_first if use_vres else None, hid, up(p["rw_w2"][j]), up(p["rw_a2"][j]),
            up(p["rw_v2"][j - 1]) if use_vres else None, up(p["rw_g2"][j]), vec, batch, seq)
    return z, p["rw_w_o"][j].astype(BF16), v_first


def _attn_layer(xb, xshape, i, j, p, btiles, tq):
    batch, seq, d = xshape
    lambda_init = 0.8 - 0.6 * math.exp(-0.3 * i)
    qkv = matmul(xb, p["da_w_qkv"][j].astype(BF16), BF16)
    o = diff_attn(qkv, btiles, p["rel_bias"], p["da_lam"][j], p["da_subln_g"][j], batch, seq, lambda_init, tq=tq)
    return o, p["da_w_o"][j].astype(BF16)


def kernel(x, ln_g, ln_b, rw_mu, rw_w_rkv, rw_w0, rw_w1, rw_w2, rw_a0, rw_a1, rw_a2, rw_v0, rw_v1, rw_v2, rw_g1, rw_g2, rw_k_k, rw_k_a, rw_r_k, rw_lnx_g, rw_lnx_b, rw_w_o, da_w_qkv, da_lam, da_subln_g, da_w_o, rel_bias, ff_w_up, ff_conv_w, ff_conv_b, ff_w_down):
    p = dict(rw_mu=rw_mu, rw_w_rkv=rw_w_rkv, rw_w0=rw_w0, rw_w1=rw_w1, rw_w2=rw_w2, rw_a0=rw_a0, rw_a1=rw_a1,
             rw_a2=rw_a2, rw_v0=rw_v0, rw_v1=rw_v1, rw_v2=rw_v2, rw_g1=rw_g1, rw_g2=rw_g2, rw_k_k=rw_k_k,
             rw_k_a=rw_k_a, rw_r_k=rw_r_k, rw_lnx_g=rw_lnx_g, rw_lnx_b=rw_lnx_b, rw_w_o=rw_w_o,
             da_w_qkv=da_w_qkv, da_lam=da_lam, da_subln_g=da_subln_g, da_w_o=da_w_o, rel_bias=rel_bias)
    batch, seq, d = x.shape
    xshape = (batch, seq, d)
    f = ff_conv_b.shape[1]
    fp = -(-f // FFN_TILE) * FFN_TILE
    tq = min(ATTN_TILE, seq)
    btiles = bias_tiles(rel_bias, tq, tq)
    xf = x.reshape(batch * seq, d)
    xb = xf.astype(BF16)
    v_first = None
    for i in range(DEPTH):
        j = i // 2
        if i % 2 == 0:
            z, w_o, v_first = _rwkv_layer(xf, xshape, j, p, v_first)
        else:
            z, w_o = _attn_layer(xb, xshape, i, j, p, btiles, tq)
        xf, xb = proj_ln(z, w_o, xf, ln_g[i, 0], ln_b[i, 0])
        wu = _pad_to(ff_w_up[i][:, :f].astype(BF16), 1, fp)
        wg = _pad_to(ff_w_up[i][:, f:].astype(BF16), 1, fp)
        cw = _pad_to(_pad_to(ff_conv_w[i], 1, fp), 0, SUBLANES_F32)
        cb = _pad_to(ff_conv_b[i][None, :], 1, fp)
        h = ffn_up(xb, wu, wg, cw, cb, seq, tn=FFN_TILE)
        wd = _pad_to(ff_w_down[i].astype(BF16), 0, fp)
        xf, xb = proj_ln(h, wd, xf, ln_g[i, 1], ln_b[i, 1], tk=fp // FFN_DOWN_KSTEPS)
    return xf.reshape(batch, seq, d)
```

```python
import functools
import math

import numpy as np
import jax
import jax.numpy as jnp
from jax import lax
from jax.experimental import pallas as pl
from jax.experimental.pallas import tpu as pltpu

F32 = jnp.float32
BF16 = jnp.bfloat16

DEPTH = 4
RWKV_HEAD = 64
GN_EPS = 64e-5
DIFF_QK = 128
DIFF_V = 256
REL_BUCKETS = 32
REL_MAX_EXACT = 16
REL_MAX_DIST = 128
CONV_W = 3
LN_EPS = 1e-5
ALPHA = (2 * DEPTH) ** 0.25
NEG_INF = -1e30
LOG2E = math.log2(math.e)

LANES = 128
SUBLANES_F32 = 8
SUBLANES_BF16 = 16
VMEM_LIMIT_BYTES = 60000 * 1024

WKV_CHUNK = 64
LORA_PAD = 128
ATTN_TILE = 512
FFN_TILE = 512
FFN_SUBTILE = 256
FFN_SUBROWS = 512
FFN_DOWN_KSTEPS = 4
PROJ_SUBROWS = 256


def _cparams(sem):
    return pltpu.CompilerParams(dimension_semantics=sem, vmem_limit_bytes=VMEM_LIMIT_BYTES)


def _dot(a, b):
    return jnp.dot(a, b, preferred_element_type=F32)


def _dot_nt(a, b):
    return lax.dot_general(a, b, (((1,), (1,)), ((), ())), preferred_element_type=F32)


def _dot_tn(a, b):
    return lax.dot_general(a, b, (((0,), (0,)), ((), ())), preferred_element_type=F32)


def _split2(x):
    hi = x.astype(BF16)
    lo = (x - hi.astype(F32)).astype(BF16)
    return hi, lo


def _split3(x):
    hi = x.astype(BF16)
    r1 = x - hi.astype(F32)
    mid = r1.astype(BF16)
    lo = (r1 - mid.astype(F32)).astype(BF16)
    return hi, mid, lo


def _layer_norm(y, g, b):
    mu = jnp.mean(y, axis=-1, keepdims=True)
    yc = y - mu
    var = jnp.mean(yc * yc, axis=-1, keepdims=True)
    return yc * lax.rsqrt(var + LN_EPS) * g + b


def _mm_kernel(a_ref, w_ref, o_ref):
    o_ref[...] = _dot(a_ref[...], w_ref[...]).astype(o_ref.dtype)


def matmul(a, w, out_dtype, tm=1024, tn=1024):
    m, k = a.shape
    n = w.shape[1]
    tm, tn = min(tm, m), min(tn, n)
    assert m % tm == 0 and n % tn == 0
    return pl.pallas_call(
        _mm_kernel,
        grid=(m // tm, n // tn),
        in_specs=[pl.BlockSpec((tm, k), lambda i, j: (i, 0)),
                  pl.BlockSpec((k, tn), lambda i, j: (0, j))],
        out_specs=pl.BlockSpec((tm, tn), lambda i, j: (i, j)),
        out_shape=jax.ShapeDtypeStruct((m, n), out_dtype),
        compiler_params=_cparams(("parallel", "parallel")),
        name="matmul",
    )(a, w)


def _proj_ln_kernel(a_ref, w_ref, res_ref, g_ref, b_ref, xo_ref, xb_ref, *scratch, nk):
    tm = a_ref.shape[0]
    tr = min(PROJ_SUBROWS, tm)
    row_groups = [slice(r0, r0 + tr) for r0 in range(0, tm, tr)]

    def finish(partial):
        accs = [partial(rs) for rs in row_groups]
        for rs, acc in zip(row_groups, accs):
            o = _layer_norm(ALPHA * res_ref[rs, :] + acc, g_ref[...], b_ref[...])
            xo_ref[rs, :] = o
            xb_ref[rs, :] = o.astype(BF16)

    if nk == 1:
        finish(lambda rs: _dot(a_ref[rs, :], w_ref[...]))
        return
    acc_ref, = scratch
    k = pl.program_id(1)

    @pl.when(k == 0)
    def _():
        acc_ref[...] = _dot(a_ref[...], w_ref[...])

    @pl.when((k > 0) & (k < nk - 1))
    def _():
        acc_ref[...] += _dot(a_ref[...], w_ref[...])

    @pl.when(k == nk - 1)
    def _():
        finish(lambda rs: acc_ref[rs, :] + _dot(a_ref[rs, :], w_ref[...]))


def proj_ln(a, w, res, g, b, tm=512, tk=None):
    m, kdim = a.shape
    d = w.shape[1]
    tm = min(tm, m)
    tk = kdim if tk is None else tk
    assert m % tm == 0 and kdim % tk == 0
    nk = kdim // tk
    return pl.pallas_call(
        functools.partial(_proj_ln_kernel, nk=nk),
        grid=(m // tm, nk),
        in_specs=[pl.BlockSpec((tm, tk), lambda i, k: (i, k)),
                  pl.BlockSpec((tk, d), lambda i, k: (k, 0)),
                  pl.BlockSpec((tm, d), lambda i, k: (i, 0)),
                  pl.BlockSpec((1, d), lambda i, k: (0, 0)),
                  pl.BlockSpec((1, d), lambda i, k: (0, 0))],
        out_specs=[pl.BlockSpec((tm, d), lambda i, k: (i, 0)),
                   pl.BlockSpec((tm, d), lambda i, k: (i, 0))],
        out_shape=[jax.ShapeDtypeStruct((m, d), F32), jax.ShapeDtypeStruct((m, d), BF16)],
        scratch_shapes=[pltpu.VMEM((tm, d), F32)] if nk > 1 else [],
        compiler_params=_cparams(("parallel", "arbitrary")),
        name="proj_ln",
    )(a, w, res, g.reshape(1, d), b.reshape(1, d))


def _ffn_up_kernel(x_ref, xh_ref, wu_ref, wg_ref, cw_ref, cb_ref, h_ref, *, tm, seq):
    i = pl.program_id(0)
    x = x_ref[...]
    first = (i * tm) % seq == 0
    halo = jnp.where(first, jnp.zeros_like(xh_ref[...]), xh_ref[...])
    xe = jnp.concatenate([halo, x], axis=0)
    hl = halo.shape[0]
    tn = h_ref.shape[1]
    tr = min(FFN_SUBROWS, tm)
    subs = [(slice(r0, r0 + tr), slice(c0, min(c0 + FFN_SUBTILE, tn)))
            for r0 in range(0, tm, tr) for c0 in range(0, tn, FFN_SUBTILE)]
    ug = [(_dot(x[rs], wu_ref[:, cs]), _dot(xe[rs.start:rs.stop + hl], wg_ref[:, cs])) for rs, cs in subs]
    for (rs, cs), (u, ge) in zip(subs, ug):
        g0 = ge[hl:, :]
        g1 = pltpu.roll(ge, 1, axis=0)[hl:, :]
        g2 = pltpu.roll(ge, 2, axis=0)[hl:, :]
        gc = cb_ref[:, cs] + g2 * cw_ref[0:1, cs] + g1 * cw_ref[1:2, cs] + g0 * cw_ref[2:3, cs]
        h_ref[rs, cs] = (u * jax.nn.gelu(gc)).astype(h_ref.dtype)


def ffn_up(xb, wu, wg, conv_w, conv_b, seq, tm=1024, tn=512):
    m, d = xb.shape
    fp = wu.shape[1]
    tm, tn = min(tm, seq), min(tn, fp)
    hl = SUBLANES_BF16
    assert m % tm == 0 and fp % tn == 0 and seq % tm == 0 and tm % hl == 0
    r = tm // hl
    return pl.pallas_call(
        functools.partial(_ffn_up_kernel, tm=tm, seq=seq),
        grid=(m // tm, fp // tn),
        in_specs=[pl.BlockSpec((tm, d), lambda i, j: (i, 0)),
                  pl.BlockSpec((hl, d), lambda i, j: (jnp.maximum(i * r - 1, 0), 0)),
                  pl.BlockSpec((d, tn), lambda i, j: (0, j)),
                  pl.BlockSpec((d, tn), lambda i, j: (0, j)),
                  pl.BlockSpec((SUBLANES_F32, tn), lambda i, j: (0, j)),
                  pl.BlockSpec((1, tn), lambda i, j: (0, j))],
        out_specs=pl.BlockSpec((tm, tn), lambda i, j: (i, j)),
        out_shape=jax.ShapeDtypeStruct((m, fp), BF16),
        compiler_params=_cparams(("parallel", "parallel")),
        name="ffn_up",
    )(xb, xb, wu, wg, conv_w, conv_b)


def _bucket_thresholds():
    n = np.arange(1, 4 * REL_MAX_DIST, dtype=np.float64)
    large = REL_MAX_EXACT + np.floor(np.log(n / REL_MAX_EXACT) / math.log(REL_MAX_DIST / REL_MAX_EXACT)
                                     * (REL_BUCKETS - REL_MAX_EXACT)).astype(np.int64)
    bucket = np.where(n < REL_MAX_EXACT, n.astype(np.int64), np.minimum(large, REL_BUCKETS - 1))
    thr = [int(n[i]) for i in range(len(n)) if i == 0 or bucket[i] != bucket[i - 1]]
    assert bucket[0] == 1 and len(thr) == REL_BUCKETS - 1 and np.all(np.diff(bucket) >= 0)
    return thr


_BUCKET_THR = _bucket_thresholds()
REL_FAR = _BUCKET_THR[-1]


def _bias_tile_kernel(rb_ref, o_ref, *, tq, tk):
    h = pl.program_id(0)
    w = pl.program_id(1)
    row = lax.broadcasted_iota(jnp.int32, (tq, tk), 0)
    col = lax.broadcasted_iota(jnp.int32, (tq, tk), 1)
    dist = w * tk + row - col
    val = jnp.full((tq, tk), rb_ref[0, h], F32)
    for m, t in enumerate(_BUCKET_THR):
        val = jnp.where(dist >= t, rb_ref[m + 1, h], val)
    o_ref[0, 0] = jnp.where(dist >= 0, val * LOG2E, NEG_INF)


def bias_tiles(rel_bias, tq, tk):
    nh = rel_bias.shape[1]
    return pl.pallas_call(
        functools.partial(_bias_tile_kernel, tq=tq, tk=tk),
        grid=(nh, 2),
        in_specs=[pl.BlockSpec(memory_space=pltpu.SMEM)],
        out_specs=pl.BlockSpec((1, 1, tq, tk), lambda h, w: (h, w, 0, 0)),
        out_shape=jax.ShapeDtypeStruct((nh, 2, tq, tk), F32),
        compiler_params=_cparams(("parallel", "parallel")),
        name="bias_tiles",
    )(rel_bias)


def _diff_attn_kernel(rb_ref, q_ref, k_ref, v_ref, bias_ref, lam_ref, sg_ref, o_ref,
                      acc_ref, s_ref, p_ref, *, tq, tk, lambda_init):
    h = pl.program_id(1)
    qi = pl.program_id(2)
    q = q_ref[...].astype(F32) * (DIFF_QK ** -0.5 * LOG2E)
    qs = (q[:, :DIFF_QK].astype(BF16), q[:, DIFF_QK:].astype(BF16))
    acc_ref[...] = jnp.zeros_like(acc_ref)
    far_bias = rb_ref[REL_BUCKETS - 1, h] * LOG2E

    def scores(c, first_blk, nb, buf):
        start = pl.multiple_of(first_blk * tk, tk)
        s_ref[buf, c, :, :nb * tk] = _dot_nt(qs[c], k_ref[pl.ds(start, nb * tk), c * DIFF_QK:(c + 1) * DIFF_QK])

    def softmax_pv(c, first_blk, nb, buf, carry, near):
        w = nb * tk
        start = pl.multiple_of(first_blk * tk, tk)
        m, l = carry[2 * c], carry[2 * c + 1]
        s = s_ref[buf, c, :, :w]
        if near:
            s = s + (bias_ref[0, 0] if nb == 1 else jnp.concatenate([bias_ref[0, 1], bias_ref[0, 0]], axis=1))
            m_new = jnp.maximum(m, jnp.max(s, axis=-1, keepdims=True))
            shift = m_new
        else:
            m_new = jnp.maximum(m, jnp.max(s, axis=-1, keepdims=True) + far_bias)
            shift = m_new - far_bias
        alpha = jnp.exp2(m - m_new)
        p = jnp.exp2(s - shift)
        l_new = alpha * l + jnp.sum(p, axis=-1, keepdims=True)
        p_ref[c, :, :w] = p.astype(BF16)
        acc_ref[c] = alpha * acc_ref[c] + _dot(p_ref[c, :, :w], v_ref[pl.ds(start, w), :])
        return carry[:2 * c] + (m_new, l_new) + carry[2 * c + 2:]

    def unpipelined(first_blk, nb, carry, near):
        for c in range(2):
            scores(c, first_blk, nb, 0)
        for c in range(2):
            carry = softmax_pv(c, first_blk, nb, 0, carry, near)
        return carry

    odd = (qi + 1) & 1
    n_pairs = (qi + 1) >> 1
    carry = (jnp.full((tq, 1), NEG_INF, F32), jnp.zeros((tq, 1), F32),
             jnp.full((tq, 1), NEG_INF, F32), jnp.zeros((tq, 1), F32))
    carry = lax.cond(odd == 1,
                     lambda c: lax.cond(qi == 0, lambda d: unpipelined(0, 1, d, True),
                                        lambda d: unpipelined(0, 1, d, False), c),
                     lambda c: c, carry)

    def far_pair(blk, buf, carry):
        scores(0, blk + 2, 2, 1 - buf)
        carry = softmax_pv(0, blk, 2, buf, carry, False)
        scores(1, blk + 2, 2, 1 - buf)
        return softmax_pv(1, blk, 2, buf, carry, False)

    def near_pair(buf, carry):
        for c in range(2):
            carry = softmax_pv(c, qi - 1, 2, buf, carry, True)
        return carry

    def pairs(carry):
        for c in range(2):
            scores(c, odd, 2, 0)
        n_far = n_pairs - 1
        carry = lax.fori_loop(0, n_far >> 1,
                              lambda i, c: far_pair(odd + 4 * i + 2, 1, far_pair(odd + 4 * i, 0, c)), carry)
        return lax.cond((n_far & 1) == 1,
                        lambda c: near_pair(1, far_pair(odd + 2 * (n_far - 1), 0, c)),
                        lambda c: near_pair(0, c), carry)

    m0, l0, m1, l1 = lax.cond(n_pairs >= 1, pairs, lambda c: c, carry)

    lam = lam_ref[...]
    lam_full = (jnp.exp(jnp.sum(lam[0:1, :] * lam[1:2, :], axis=-1, keepdims=True))
                - jnp.exp(jnp.sum(lam[2:3, :] * lam[3:4, :], axis=-1, keepdims=True)) + lambda_init)
    o = acc_ref[0] / l0 - lam_full * (acc_ref[1] / l1)
    o = o * lax.rsqrt(jnp.mean(o * o, axis=-1, keepdims=True) + LN_EPS) * sg_ref[...] * (1.0 - lambda_init)
    o_ref[...] = o.astype(o_ref.dtype)


def diff_attn(qkv, btiles, rel_bias, lam, subln_g, batch, seq, lambda_init, tq=512):
    m, d3 = qkv.shape
    d = d3 // 3
    nh = d // DIFF_V
    tq = min(tq, seq)
    tk = tq
    assert seq % tq == 0 and tk >= REL_FAR and btiles.shape == (nh, 2, tq, tk)
    nq = seq // tq
    return pl.pallas_call(
        functools.partial(_diff_attn_kernel, tq=tq, tk=tk, lambda_init=lambda_init),
        grid=(batch, nh, nq),
        in_specs=[pl.BlockSpec(memory_space=pltpu.SMEM),
                  pl.BlockSpec((tq, DIFF_V), lambda b, h, i: (b * nq + i, h)),
                  pl.BlockSpec((seq, DIFF_V), lambda b, h, i: (b, nh + h)),
                  pl.BlockSpec((seq, DIFF_V), lambda b, h, i: (b, 2 * nh + h)),
                  pl.BlockSpec((1, 2, tq, tk), lambda b, h, i: (h, 0, 0, 0)),
                  pl.BlockSpec((4, DIFF_QK), lambda b, h, i: (0, 0)),
                  pl.BlockSpec((1, DIFF_V), lambda b, h, i: (0, 0))],
        out_specs=pl.BlockSpec((tq, DIFF_V), lambda b, h, i: (b * nq + i, h)),
        out_shape=jax.ShapeDtypeStruct((m, d), BF16),
        scratch_shapes=[pltpu.VMEM((2, tq, DIFF_V), F32),
                        pltpu.VMEM((2, 2, tq, 2 * tk), F32),
                        pltpu.VMEM((2, tq, 2 * tk), BF16)],
        compiler_params=_cparams(("parallel", "parallel", "arbitrary")),
        name="diff_attn",
    )(rel_bias, qkv, qkv, qkv, btiles, lam, subln_g.reshape(1, DIFF_V))


def _shift_delta(x_ref, xh_ref, tm, seq, i):
    x = x_ref[...]
    first = (i * tm) % seq == 0
    prev = jnp.where(first, jnp.zeros((1, x.shape[1]), F32), xh_ref[SUBLANES_F32 - 1:SUBLANES_F32, :])
    row = lax.broadcasted_iota(jnp.int32, x.shape, 0)
    return x, jnp.where(row == 0, prev, pltpu.roll(x, 1, axis=0)) - x


def _rwkv_rkv_kernel(x_ref, xh_ref, mu_ref, w_ref, o_ref, *, tm, seq):
    x, xx = _shift_delta(x_ref, xh_ref, tm, seq, pl.program_id(1))
    xin = (x + xx * mu_ref[0]).astype(BF16)
    o_ref[0] = _dot(xin, w_ref[0])


def rwkv_rkv(x, mu3, wrkv, seq, tm=512):
    m, d = x.shape
    tm = min(tm, seq)
    assert m % tm == 0 and seq % tm == 0
    r = tm // SUBLANES_F32
    return pl.pallas_call(
        functools.partial(_rwkv_rkv_kernel, tm=tm, seq=seq),
        grid=(3, m // tm),
        in_specs=[pl.BlockSpec((tm, d), lambda g, i: (i, 0)),
                  pl.BlockSpec((SUBLANES_F32, d), lambda g, i: (jnp.maximum(i * r - 1, 0), 0)),
                  pl.BlockSpec((1, 1, d), lambda g, i: (g, 0, 0)),
                  pl.BlockSpec((1, d, d), lambda g, i: (g, 0, 0))],
        out_specs=pl.BlockSpec((1, tm, d), lambda g, i: (g, i, 0)),
        out_shape=jax.ShapeDtypeStruct((3, m, d), F32),
        compiler_params=_cparams(("parallel", "parallel")),
        name="rwkv_rkv",
    )(x, x, mu3, wrkv)


def _rwkv_lora_kernel(x_ref, xh_ref, mu_ref, wl_ref, hid_ref, *, tm, seq):
    x, xx = _shift_delta(x_ref, xh_ref, tm, seq, pl.program_id(0))

    def lora(j):
        xin = (x + xx * mu_ref[j:j + 1, :]).astype(BF16)
        return _dot(xin, wl_ref[:, j * LORA_PAD:(j + 1) * LORA_PAD])

    hid_ref[:, 0 * LORA_PAD:1 * LORA_PAD] = jnp.tanh(lora(0)).astype(BF16)
    hid_ref[:, 1 * LORA_PAD:2 * LORA_PAD] = lora(1).astype(BF16)
    hid_ref[:, 2 * LORA_PAD:3 * LORA_PAD] = lora(2).astype(BF16)
    hid_ref[:, 3 * LORA_PAD:4 * LORA_PAD] = jax.nn.sigmoid(lora(3)).astype(BF16)


def rwkv_lora(x, mu4, wl, seq, tm=512):
    m, d = x.shape
    tm = min(tm, seq)
    assert m % tm == 0 and seq % tm == 0
    r = tm // SUBLANES_F32
    return pl.pallas_call(
        functools.partial(_rwkv_lora_kernel, tm=tm, seq=seq),
        grid=(m // tm,),
        in_specs=[pl.BlockSpec((tm, d), lambda i: (i, 0)),
                  pl.BlockSpec((SUBLANES_F32, d), lambda i: (jnp.maximum(i * r - 1, 0), 0)),
                  pl.BlockSpec((SUBLANES_F32, d), lambda i: (0, 0)),
                  pl.BlockSpec((d, 4 * LORA_PAD), lambda i: (0, 0))],
        out_specs=pl.BlockSpec((tm, 4 * LORA_PAD), lambda i: (i, 0)),
        out_shape=jax.ShapeDtypeStruct((m, 4 * LORA_PAD), BF16),
        compiler_params=_cparams(("parallel",)),
        name="rwkv_lora",
    )(x, x, mu4, wl)


def _softplus(x):
    return jnp.maximum(x, 0.0) + jnp.log(1.0 + jnp.exp(-jnp.abs(x)))


def _wkv_masks(tb):
    lc = WKV_CHUNK
    ri, ci = np.indices((tb, tb))
    same_chunk = (ri // lc) == (ci // lc)
    tri = [same_chunk & (ri > ci), same_chunk & (ri >= ci)]
    lev = [((ri >> (lb + 1)) == (ci >> (lb + 1))) & (((ri >> lb) & 1) == 1) & (((ci >> lb) & 1) == 0)
           for lb in range(lc.bit_length() - 1)]
    return np.stack(tri).astype(np.float32), np.stack(lev).astype(np.float32)


def _wkv_kernel(*refs, use_vres, tb, pairs):
    nin = 12 if use_vres else 10
    ins, z_ref, (s_ref, rq_ref, y1_ref, d_ref, c_ref, gam_ref, g_ref, bon_ref) = refs[:nin], refs[nin], refs[nin + 1:]
    if use_vres:
        m_ref, ml_ref, r_ref, k_ref, v_ref, vf_ref, hid_ref, w2_ref, a2_ref, v2_ref, g2_ref, vec_ref = ins
    else:
        m_ref, ml_ref, r_ref, k_ref, v_ref, hid_ref, w2_ref, a2_ref, g2_ref, vec_ref = ins
    n = RWKV_HEAD
    lc = WKV_CHUNK
    nlev = lc.bit_length() - 1
    nc = tb // lc

    @pl.when(pl.program_id(2) == 0)
    def _():
        for ref in (s_ref, rq_ref, y1_ref, d_ref, c_ref, gam_ref, g_ref, bon_ref):
            ref[...] = jnp.zeros_like(ref)

    lane = lax.broadcasted_iota(jnp.int32, (1, LANES), 1)
    head_masks = (lane < n, lane >= n)
    pr = lax.broadcasted_iota(jnp.int32, (LANES, LANES), 0)
    pc = lax.broadcasted_iota(jnp.int32, (LANES, LANES), 1)
    same_head = jnp.where((pr >= n) == (pc >= n), 1.0, 0.0)
    p_sum = same_head.astype(BF16)
    p_mean = (same_head * (1.0 / n)).astype(BF16)

    def head_reduce(x, p):
        return _dot(x.astype(BF16), p)

    hid = hid_ref[...]
    strict, incl = m_ref[0], m_ref[1]
    tril = incl.astype(BF16)
    lanes = [slice(p * LANES, (p + 1) * LANES) for p in range(pairs)]

    prev = [dict(rqb=rq_ref[p], y1=y1_ref[p], gam=gam_ref[p], g=g_ref[p], bonus=bon_ref[p], s=s_ref[p], ys=[])
            for p in range(pairs)]

    def recurrence_step(c):
        sl = slice(c * lc, (c + 1) * lc)
        for p, t in enumerate(prev):
            sb = t["s"].astype(BF16)
            t["ys"].append(_dot_nt(t["rqb"][sl], sb) + t["y1"][sl])
            t["s"] = (t["s"] + _dot(sb, d_ref[p, c]) + c_ref[p, c]) * t["gam"][c:c + 1, :]

    tok = []
    for ln in lanes:
        if len(tok) < nc:
            recurrence_step(len(tok))
        r, k, v = r_ref[0, :, ln], k_ref[0, :, ln], v_ref[0, :, ln]
        vec = vec_ref[:, ln]
        w0, a0, v0, k_k, k_a, lnx_g, lnx_b, r_k = (vec[j:j + 1, :] for j in range(8))
        wlog = -_softplus(-(w0 + _dot(hid[:, 0 * LORA_PAD:1 * LORA_PAD], w2_ref[:, ln]))) - 0.5
        ld = -jnp.exp(wlog)
        a = jax.nn.sigmoid(a0 + _dot(hid[:, 1 * LORA_PAD:2 * LORA_PAD], a2_ref[:, ln]))
        g = _dot(hid[:, 3 * LORA_PAD:4 * LORA_PAD], g2_ref[:, ln])
        if use_vres:
            v = v + (vf_ref[0, :, ln] - v) * jax.nn.sigmoid(v0 + _dot(hid[:, 2 * LORA_PAD:3 * LORA_PAD], v2_ref[:, ln]))
        kk = k * k_k
        kk = kk / jnp.maximum(jnp.sqrt(head_reduce(kk * kk, p_sum)), 1e-12)
        k2 = k * (1.0 + (a - 1.0) * k_a)
        hi, lo = _split2(ld)
        cs = _dot(tril, hi) + _dot(tril, lo)
        e_pos = jnp.exp(cs)
        e_neg = jnp.exp(-cs)
        tok.append(dict(rt=r * e_pos, at=-kk * jnp.exp(cs - ld), btb=(kk * a * e_neg).astype(BF16),
                        ktb=(k2 * e_neg).astype(BF16), vb=v.astype(BF16), e_pos=e_pos, g=g,
                        bonus=head_reduce(r * k2 * r_k, p_sum) * v))

    for c in range(pairs, nc):
        recurrence_step(c)

    for p, (ln, t) in enumerate(zip(lanes, prev)):
        s_ref[p] = t["s"]
        y = jnp.concatenate(t["ys"], axis=0)
        yc = y - head_reduce(y, p_mean)
        yn = yc * lax.rsqrt(head_reduce(yc * yc, p_mean) + GN_EPS)
        z_ref[:, ln] = ((yn * vec_ref[5:6, ln] + vec_ref[6:7, ln] + t["bonus"]) * t["g"]).astype(z_ref.dtype)

    chains = []
    for t in tok:
        rhs = jnp.concatenate([t["btb"], t["ktb"]], axis=0)
        t["y0"] = jnp.zeros((tb, LANES), F32)
        for hm in head_masks:
            at_h = jnp.where(hm, t["at"], 0.0)
            rt_h = jnp.where(hm, t["rt"], 0.0)
            amat = _dot_nt(jnp.concatenate([at_h, rt_h], axis=0).astype(BF16), rhs)
            a_k = jnp.concatenate([amat[:tb, tb:] * strict, amat[tb:, tb:] * incl], axis=0).astype(BF16)
            qy = _dot(a_k, t["vb"])
            t["y0"] = t["y0"] + jnp.where(hm, qy[tb:], 0.0)
            chains.append(dict(tok=t, hm=hm, a_rb=(amat[tb:, :tb] * incl).astype(BF16),
                               g=(amat[:tb, :tb] * strict).astype(BF16),
                               rhs=jnp.concatenate([at_h, jnp.where(hm, qy[:tb], 0.0)], axis=1)))
    for lb in range(nlev):
        for ch in chains:
            g = ch["g"]
            ch["g"] = (g.astype(F32) + _dot(g * ml_ref[lb], g)).astype(BF16)
    for t in tok:
        t["w_uv"] = jnp.zeros((tb, 2 * LANES), F32)
    for ch in chains:
        t = ch["tok"]
        t["w_uv"] = t["w_uv"] + ch["rhs"] + _dot(ch["g"], ch["rhs"].astype(BF16))
    for t in tok:
        t["w_uvb"] = t["w_uv"].astype(BF16)
        t["rq"], t["y1"] = t["rt"], t["y0"]
    for ch in chains:
        t = ch["tok"]
        x = _dot(ch["a_rb"], t["w_uvb"])
        t["rq"] = t["rq"] + jnp.where(ch["hm"], x[:, :LANES], 0.0)
        t["y1"] = t["y1"] + jnp.where(ch["hm"], x[:, LANES:], 0.0)

    trans = []
    for t in tok:
        for c in range(nc):
            sl = slice(c * lc, (c + 1) * lc)
            d_c = (_dot_tn(t["w_uvb"][sl, :LANES], t["btb"][sl]) * same_head).astype(BF16)
            c_c = _dot_tn(jnp.concatenate([t["w_uvb"][sl, LANES:], t["vb"][sl]], axis=0),
                          jnp.concatenate([t["btb"][sl], t["ktb"][sl]], axis=0)) * same_head
            trans.append((d_c, c_c))

    for p, t in enumerate(tok):
        rq_ref[p] = t["rq"].astype(BF16)
        y1_ref[p] = t["y1"]
        g_ref[p] = t["g"]
        bon_ref[p] = t["bonus"]
        for c in range(nc):
            d_ref[p, c], c_ref[p, c] = trans[p * nc + c]
            gam_ref[p, c:c + 1, :] = t["e_pos"][(c + 1) * lc - 1:(c + 1) * lc, :]


def wkv(rkv, vfirst, hid, w2, a2, v2, g2, vec, batch, seq, tb=256, pairs=4):
    _, m, d = rkv.shape
    tb = min(tb, seq)
    pairs = min(pairs, d // LANES)
    wl = pairs * LANES
    assert seq % tb == 0 and tb % WKV_CHUNK == 0 and d % wl == 0
    nt = seq // tb
    use_vres = vfirst is not None
    tri, lev = _wkv_masks(tb)
    masks, lev_masks = jnp.asarray(tri), jnp.asarray(lev, dtype=BF16)
    nc = tb // WKV_CHUNK
    blk_in = lambda b, t: b * nt + jnp.minimum(t, nt - 1)
    lora_spec = pl.BlockSpec((LORA_PAD, wl), lambda b, p, t: (0, p))
    in_specs = [pl.BlockSpec(masks.shape, lambda b, p, t: (0, 0, 0)),
                pl.BlockSpec(lev_masks.shape, lambda b, p, t: (0, 0, 0)),
                pl.BlockSpec((1, tb, wl), lambda b, p, t: (0, blk_in(b, t), p)),
                pl.BlockSpec((1, tb, wl), lambda b, p, t: (1, blk_in(b, t), p)),
                pl.BlockSpec((1, tb, wl), lambda b, p, t: (2, blk_in(b, t), p))]
    args = [masks, lev_masks, rkv, rkv, rkv]
    if use_vres:
        in_specs.append(pl.BlockSpec((1, tb, wl), lambda b, p, t: (2, blk_in(b, t), p)))
        args.append(vfirst)
    in_specs.append(pl.BlockSpec((tb, 4 * LORA_PAD), lambda b, p, t: (blk_in(b, t), 0)))
    args.append(hid)
    for wgt in ((w2, a2, v2, g2) if use_vres else (w2, a2, g2)):
        in_specs.append(lora_spec)
        args.append(wgt)
    in_specs.append(pl.BlockSpec((SUBLANES_F32, wl), lambda b, p, t: (0, p)))
    args.append(vec)
    return pl.pallas_call(
        functools.partial(_wkv_kernel, use_vres=use_vres, tb=tb, pairs=pairs),
        grid=(batch, d // wl, nt + 1),
        in_specs=in_specs,
        out_specs=pl.BlockSpec((tb, wl), lambda b, p, t: (b * nt + jnp.maximum(t - 1, 0), p)),
        out_shape=jax.ShapeDtypeStruct((m, d), BF16),
        scratch_shapes=[pltpu.VMEM((pairs, LANES, LANES), F32),
                        pltpu.VMEM((pairs, tb, LANES), BF16),
                        pltpu.VMEM((pairs, tb, LANES), F32),
                        pltpu.VMEM((pairs, nc, LANES, LANES), BF16),
                        pltpu.VMEM((pairs, nc, LANES, LANES), F32),
                        pltpu.VMEM((pairs, SUBLANES_F32, LANES), F32),
                        pltpu.VMEM((pairs, tb, LANES), F32),
                        pltpu.VMEM((pairs, tb, LANES), F32)],
        compiler_params=_cparams(("parallel", "parallel", "arbitrary")),
        name="wkv",
    )(*args)


def _pad_to(x, axis, size):
    pad = [(0, 0)] * x.ndim
    pad[axis] = (0, size - x.shape[axis])
    return jnp.pad(x, pad)


def _rwkv_layer(x, xshape, j, p, v_first):
    batch, seq, d = xshape
    mu = p["rw_mu"][j]
    mu3 = jnp.stack([mu[0], mu[2], mu[3]])[:, None, :]
    mu4 = _pad_to(jnp.stack([mu[1], mu[4], mu[3], mu[5]]), 0, SUBLANES_F32)
    use_vres = j > 0
    v1 = p["rw_v1"][j - 1] if use_vres else jnp.zeros((d, LORA_PAD), F32)
    wl = jnp.concatenate([_pad_to(w, 1, LORA_PAD) for w in (p["rw_w1"][j], p["rw_a1"][j], v1, p["rw_g1"][j])],
                         axis=1).astype(BF16)
    up = lambda w: _pad_to(w, 0, LORA_PAD).astype(BF16)
    vec = jnp.stack([p["rw_w0"][j], p["rw_a0"][j], p["rw_v0"][j - 1] if use_vres else jnp.zeros((d,), F32),
                     p["rw_k_k"][j], p["rw_k_a"][j], p["rw_lnx_g"][j], p["rw_lnx_b"][j], p["rw_r_k"][j].reshape(d)])
    rkv = rwkv_rkv(x, mu3, p["rw_w_rkv"][j].astype(BF16), seq)
    hid = rwkv_lora(x, mu4, wl, seq)
    if not use_vres:
        v_first = rkv
    z = wkv(rkv, v_first if use_vres else None, hid, up(p["rw_w2"][j]), up(p["rw_a2"][j]),
            up(p["rw_v2"][j - 1]) if use_vres else None, up(p["rw_g2"][j]), vec, batch, seq)
    return z, p["rw_w_o"][j].astype(BF16), v_first


def _attn_layer(xb, xshape, i, j, p, btiles, tq):
    batch, seq, d = xshape
    lambda_init = 0.8 - 0.6 * math.exp(-0.3 * i)
    qkv = matmul(xb, p["da_w_qkv"][j].astype(BF16), BF16)
    o = diff_attn(qkv, btiles, p["rel_bias"], p["da_lam"][j], p["da_subln_g"][j], batch, seq, lambda_init, tq=tq)
    return o, p["da_w_o"][j].astype(BF16)


def kernel(x, ln_g, ln_b, rw_mu, rw_w_rkv, rw_w0, rw_w1, rw_w2, rw_a0, rw_a1, rw_a2, rw_v0, rw_v1, rw_v2, rw_g1, rw_g2, rw_k_k, rw_k_a, rw_r_k, rw_lnx_g, rw_lnx_b, rw_w_o, da_w_qkv, da_lam, da_subln_g, da_w_o, rel_bias, ff_w_up, ff_conv_w, ff_conv_b, ff_w_down):
    p = dict(rw_mu=rw_mu, rw_w_rkv=rw_w_rkv, rw_w0=rw_w0, rw_w1=rw_w1, rw_w2=rw_w2, rw_a0=rw_a0, rw_a1=rw_a1,
             rw_a2=rw_a2, rw_v0=rw_v0, rw_v1=rw_v1, rw_v2=rw_v2, rw_g1=rw_g1, rw_g2=rw_g2, rw_k_k=rw_k_k,
             rw_k_a=rw_k_a, rw_r_k=rw_r_k, rw_lnx_g=rw_lnx_g, rw_lnx_b=rw_lnx_b, rw_w_o=rw_w_o,
             da_w_qkv=da_w_qkv, da_lam=da_lam, da_subln_g=da_subln_g, da_w_o=da_w_o, rel_bias=rel_bias)
    batch, seq, d = x.shape
    xshape = (batch, seq, d)
    f = ff_conv_b.shape[1]
    fp = -(-f // FFN_TILE) * FFN_TILE
    tq = min(ATTN_TILE, seq)
    btiles = bias_tiles(rel_bias, tq, tq)
    xf = x.reshape(batch * seq, d)
    xb = xf.astype(BF16)
    v_first = None
    for i in range(DEPTH):
        j = i // 2
        if i % 2 == 0:
            z, w_o, v_first = _rwkv_layer(xf, xshape, j, p, v_first)
        else:
            z, w_o = _attn_layer(xb, xshape, i, j, p, btiles, tq)
        xf, xb = proj_ln(z, w_o, xf, ln_g[i, 0], ln_b[i, 0])
        wu = _pad_to(ff_w_up[i][:, :f].astype(BF16), 1, fp)
        wg = _pad_to(ff_w_up[i][:, f:].astype(BF16), 1, fp)
        cw = _pad_to(_pad_to(ff_conv_w[i], 1, fp), 0, SUBLANES_F32)
        cb = _pad_to(ff_conv_b[i][None, :], 1, fp)
        h = ffn_up(xb, wu, wg, cw, cb, seq, tn=FFN_TILE)
        wd = _pad_to(ff_w_down[i].astype(BF16), 0, fp)
        xf, xb = proj_ln(h, wd, xf, ln_g[i, 1], ln_b[i, 1], tk=fp // FFN_DOWN_KSTEPS)
    return xf.reshape(batch, seq, d)
```

```python
import functools
import math

import numpy as np
import jax
import jax.numpy as jnp
from jax import lax
from jax.experimental import pallas as pl
from jax.experimental.pallas import tpu as pltpu

F32 = jnp.float32
BF16 = jnp.bfloat16

DEPTH = 4
RWKV_HEAD = 64
GN_EPS = 64e-5
DIFF_QK = 128
DIFF_V = 256
REL_BUCKETS = 32
REL_MAX_EXACT = 16
REL_MAX_DIST = 128
CONV_W = 3
LN_EPS = 1e-5
ALPHA = (2 * DEPTH) ** 0.25
NEG_INF = -1e30
LOG2E = math.log2(math.e)

LANES = 128
SUBLANES_F32 = 8
SUBLANES_BF16 = 16
VMEM_LIMIT_BYTES = 60000 * 1024

WKV_CHUNK = 64
LORA_PAD = 128
ATTN_TILE = 512
FFN_TILE = 512
FFN_SUBTILE = 256
FFN_SUBROWS = 512
FFN_DOWN_ROWS = 256
PROJ_SUBROWS = 256


def _cparams(sem):
    return pltpu.CompilerParams(dimension_semantics=sem, vmem_limit_bytes=VMEM_LIMIT_BYTES)


def _dot(a, b):
    return jnp.dot(a, b, preferred_element_type=F32)


def _dot_nt(a, b):
    return lax.dot_general(a, b, (((1,), (1,)), ((), ())), preferred_element_type=F32)


def _dot_tn(a, b):
    return lax.dot_general(a, b, (((0,), (0,)), ((), ())), preferred_element_type=F32)


def _split2(x):
    hi = x.astype(BF16)
    lo = (x - hi.astype(F32)).astype(BF16)
    return hi, lo


def _split3(x):
    hi = x.astype(BF16)
    r1 = x - hi.astype(F32)
    mid = r1.astype(BF16)
    lo = (r1 - mid.astype(F32)).astype(BF16)
    return hi, mid, lo


def _layer_norm(y, g, b):
    mu = jnp.mean(y, axis=-1, keepdims=True)
    yc = y - mu
    var = jnp.mean(yc * yc, axis=-1, keepdims=True)
    return yc * lax.rsqrt(var + LN_EPS) * g + b


def _mm_kernel(a_ref, w_ref, o_ref):
    o_ref[...] = _dot(a_ref[...], w_ref[...]).astype(o_ref.dtype)


def matmul(a, w, out_dtype, tm=1024, tn=1024):
    m, k = a.shape
    n = w.shape[1]
    tm, tn = min(tm, m), min(tn, n)
    assert m % tm == 0 and n % tn == 0
    return pl.pallas_call(
        _mm_kernel,
        grid=(m // tm, n // tn),
        in_specs=[pl.BlockSpec((tm, k), lambda i, j: (i, 0)),
                  pl.BlockSpec((k, tn), lambda i, j: (0, j))],
        out_specs=pl.BlockSpec((tm, tn), lambda i, j: (i, j)),
        out_shape=jax.ShapeDtypeStruct((m, n), out_dtype),
        compiler_params=_cparams(("parallel", "parallel")),
        name="matmul",
    )(a, w)


def _proj_ln_kernel(a_ref, w_ref, res_ref, g_ref, b_ref, xo_ref, xb_ref, *scratch, nk):
    tm = a_ref.shape[0]
    tr = min(PROJ_SUBROWS, tm)
    row_groups = [slice(r0, r0 + tr) for r0 in range(0, tm, tr)]

    def finish(partial):
        accs = [partial(rs) for rs in row_groups]
        for rs, acc in zip(row_groups, accs):
            o = _layer_norm(ALPHA * res_ref[rs, :] + acc, g_ref[...], b_ref[...])
            xo_ref[rs, :] = o
            xb_ref[rs, :] = o.astype(BF16)

    if nk == 1:
        finish(lambda rs: _dot(a_ref[rs, :], w_ref[...]))
        return
    acc_ref, = scratch
    k = pl.program_id(1)

    @pl.when(k == 0)
    def _():
        acc_ref[...] = _dot(a_ref[...], w_ref[...])

    @pl.when((k > 0) & (k < nk - 1))
    def _():
        acc_ref[...] += _dot(a_ref[...], w_ref[...])

    @pl.when(k == nk - 1)
    def _():
        finish(lambda rs: acc_ref[rs, :] + _dot(a_ref[rs, :], w_ref[...]))


def proj_ln(a, w, res, g, b, tm=512, tk=None):
    m, kdim = a.shape
    d = w.shape[1]
    tm = min(tm, m)
    tk = kdim if tk is None else tk
    assert m % tm == 0 and kdim % tk == 0
    nk = kdim // tk
    return pl.pallas_call(
        functools.partial(_proj_ln_kernel, nk=nk),
        grid=(m // tm, nk),
        in_specs=[pl.BlockSpec((tm, tk), lambda i, k: (i, k)),
                  pl.BlockSpec((tk, d), lambda i, k: (k, 0), **({"pipeline_mode": pl.Buffered(1)} if nk == 1 else {})),
                  pl.BlockSpec((tm, d), lambda i, k: (i, 0)),
                  pl.BlockSpec((1, d), lambda i, k: (0, 0)),
                  pl.BlockSpec((1, d), lambda i, k: (0, 0))],
        out_specs=[pl.BlockSpec((tm, d), lambda i, k: (i, 0)),
                   pl.BlockSpec((tm, d), lambda i, k: (i, 0))],
        out_shape=[jax.ShapeDtypeStruct((m, d), F32), jax.ShapeDtypeStruct((m, d), BF16)],
        scratch_shapes=[pltpu.VMEM((tm, d), F32)] if nk > 1 else [],
        compiler_params=_cparams(("parallel", "arbitrary")),
        name="proj_ln",
    )(a, w, res, g.reshape(1, d), b.reshape(1, d))


def _ffn_up_kernel(x_ref, xh_ref, wu_ref, wg_ref, cw_ref, cb_ref, h_ref, *, tm, seq):
    i = pl.program_id(0)
    x = x_ref[...]
    first = (i * tm) % seq == 0
    halo = jnp.where(first, jnp.zeros_like(xh_ref[...]), xh_ref[...])
    xe = jnp.concatenate([halo, x], axis=0)
    hl = halo.shape[0]
    tn = h_ref.shape[1]
    tr = min(FFN_SUBROWS, tm)
    subs = [(slice(r0, r0 + tr), slice(c0, min(c0 + FFN_SUBTILE, tn)))
            for r0 in range(0, tm, tr) for c0 in range(0, tn, FFN_SUBTILE)]
    ug = [(_dot(x[rs], wu_ref[:, cs]), _dot(xe[rs.start:rs.stop + hl], wg_ref[:, cs])) for rs, cs in subs]
    for (rs, cs), (u, ge) in zip(subs, ug):
        g0 = ge[hl:, :]
        g1 = pltpu.roll(ge, 1, axis=0)[hl:, :]
        g2 = pltpu.roll(ge, 2, axis=0)[hl:, :]
        gc = cb_ref[:, cs] + g2 * cw_ref[0:1, cs] + g1 * cw_ref[1:2, cs] + g0 * cw_ref[2:3, cs]
        h_ref[rs, cs] = (u * jax.nn.gelu(gc)).astype(h_ref.dtype)


def ffn_up(xb, wu, wg, conv_w, conv_b, seq, tm=2048, tn=512):
    m, d = xb.shape
    fp = wu.shape[1]
    tm, tn = min(tm, seq), min(tn, fp)
    hl = SUBLANES_BF16
    assert m % tm == 0 and fp % tn == 0 and seq % tm == 0 and tm % hl == 0
    r = tm // hl
    return pl.pallas_call(
        functools.partial(_ffn_up_kernel, tm=tm, seq=seq),
        grid=(m // tm, fp // tn),
        in_specs=[pl.BlockSpec((tm, d), lambda i, j: (i, 0)),
                  pl.BlockSpec((hl, d), lambda i, j: (jnp.maximum(i * r - 1, 0), 0)),
                  pl.BlockSpec((d, tn), lambda i, j: (0, j)),
                  pl.BlockSpec((d, tn), lambda i, j: (0, j)),
                  pl.BlockSpec((SUBLANES_F32, tn), lambda i, j: (0, j)),
                  pl.BlockSpec((1, tn), lambda i, j: (0, j))],
        out_specs=pl.BlockSpec((tm, tn), lambda i, j: (i, j)),
        out_shape=jax.ShapeDtypeStruct((m, fp), BF16),
        compiler_params=_cparams(("parallel", "parallel")),
        name="ffn_up",
    )(xb, xb, wu, wg, conv_w, conv_b)


def _bucket_thresholds():
    n = np.arange(1, 4 * REL_MAX_DIST, dtype=np.float64)
    large = REL_MAX_EXACT + np.floor(np.log(n / REL_MAX_EXACT) / math.log(REL_MAX_DIST / REL_MAX_EXACT)
                                     * (REL_BUCKETS - REL_MAX_EXACT)).astype(np.int64)
    bucket = np.where(n < REL_MAX_EXACT, n.astype(np.int64), np.minimum(large, REL_BUCKETS - 1))
    thr = [int(n[i]) for i in range(len(n)) if i == 0 or bucket[i] != bucket[i - 1]]
    assert bucket[0] == 1 and len(thr) == REL_BUCKETS - 1 and np.all(np.diff(bucket) >= 0)
    return thr


_BUCKET_THR = _bucket_thresholds()
REL_FAR = _BUCKET_THR[-1]


def _bias_tile_kernel(rb_ref, o_ref, *, tq, tk):
    h = pl.program_id(0)
    w = pl.program_id(1)
    row = lax.broadcasted_iota(jnp.int32, (tq, tk), 0)
    col = lax.broadcasted_iota(jnp.int32, (tq, tk), 1)
    dist = w * tk + row - col
    val = jnp.full((tq, tk), rb_ref[0, h], F32)
    for m, t in enumerate(_BUCKET_THR):
        val = jnp.where(dist >= t, rb_ref[m + 1, h], val)
    o_ref[0, 0] = jnp.where(dist >= 0, val * LOG2E, NEG_INF)


def bias_tiles(rel_bias, tq, tk):
    nh = rel_bias.shape[1]
    return pl.pallas_call(
        functools.partial(_bias_tile_kernel, tq=tq, tk=tk),
        grid=(nh, 2),
        in_specs=[pl.BlockSpec(memory_space=pltpu.SMEM)],
        out_specs=pl.BlockSpec((1, 1, tq, tk), lambda h, w: (h, w, 0, 0)),
        out_shape=jax.ShapeDtypeStruct((nh, 2, tq, tk), F32),
        compiler_params=_cparams(("parallel", "parallel")),
        name="bias_tiles",
    )(rel_bias)


def _diff_attn_kernel(rb_ref, q_ref, k_ref, v_ref, bias_ref, lam_ref, sg_ref, o_ref,
                      acc_ref, s_ref, p_ref, *, tq, tk, lambda_init):
    h = pl.program_id(1)
    qi = pl.program_id(2)
    q = q_ref[...].astype(F32) * (DIFF_QK ** -0.5 * LOG2E)
    qs = (q[:, :DIFF_QK].astype(BF16), q[:, DIFF_QK:].astype(BF16))
    acc_ref[...] = jnp.zeros_like(acc_ref)
    far_bias = rb_ref[REL_BUCKETS - 1, h] * LOG2E

    def scores(c, first_blk, nb, buf):
        start = pl.multiple_of(first_blk * tk, tk)
        s_ref[buf, c, :, :nb * tk] = _dot_nt(qs[c], k_ref[pl.ds(start, nb * tk), c * DIFF_QK:(c + 1) * DIFF_QK])

    def softmax(c, nb, buf, carry, near):
        w = nb * tk
        m, l = carry[2 * c], carry[2 * c + 1]
        s = s_ref[buf, c, :, :w]
        if near:
            s = s + (bias_ref[0, 0] if nb == 1 else jnp.concatenate([bias_ref[0, 1], bias_ref[0, 0]], axis=1))
            m_new = jnp.maximum(m, jnp.max(s, axis=-1, keepdims=True))
            shift = m_new
        else:
            m_new = jnp.maximum(m, jnp.max(s, axis=-1, keepdims=True) + far_bias)
            shift = m_new - far_bias
        alpha = jnp.exp2(m - m_new)
        p = jnp.exp2(s - shift)
        l_new = alpha * l + jnp.sum(p, axis=-1, keepdims=True)
        p_ref[c, :, :w] = p.astype(BF16)
        return carry[:2 * c] + (m_new, l_new) + carry[2 * c + 2:], alpha

    def values(c, first_blk, nb, alpha):
        w = nb * tk
        start = pl.multiple_of(first_blk * tk, tk)
        acc_ref[c] = alpha * acc_ref[c] + _dot(p_ref[c, :, :w], v_ref[pl.ds(start, w), :])

    def unpipelined(first_blk, nb, carry, near):
        for c in range(2):
            scores(c, first_blk, nb, 0)
        for c in range(2):
            carry, alpha = softmax(c, nb, 0, carry, near)
            values(c, first_blk, nb, alpha)
        return carry

    odd = (qi + 1) & 1
    n_pairs = (qi + 1) >> 1
    carry = (jnp.full((tq, 1), NEG_INF, F32), jnp.zeros((tq, 1), F32),
             jnp.full((tq, 1), NEG_INF, F32), jnp.zeros((tq, 1), F32))
    carry = lax.cond(odd == 1,
                     lambda c: lax.cond(qi == 0, lambda d: unpipelined(0, 1, d, True),
                                        lambda d: unpipelined(0, 1, d, False), c),
                     lambda c: c, carry)

    def far_pair(blk, buf, carry):
        for c in range(2):
            scores(c, blk + 2, 2, 1 - buf)
            carry, alpha = softmax(c, 2, buf, carry, False)
            values(c, blk, 2, alpha)
        return carry

    def near_pair(buf, carry):
        for c in range(2):
            carry, alpha = softmax(c, 2, buf, carry, True)
            values(c, qi - 1, 2, alpha)
        return carry

    def pairs(carry):
        for c in range(2):
            scores(c, odd, 2, 0)
        n_far = n_pairs - 1
        carry = lax.fori_loop(0, n_far >> 1,
                              lambda i, c: far_pair(odd + 4 * i + 2, 1, far_pair(odd + 4 * i, 0, c)), carry)
        return lax.cond((n_far & 1) == 1,
                        lambda c: near_pair(1, far_pair(odd + 2 * (n_far - 1), 0, c)),
                        lambda c: near_pair(0, c), carry)

    carry = lax.cond(n_pairs >= 1, pairs, lambda c: c, carry)
    l0, l1 = carry[1], carry[3]

    lam = lam_ref[...]
    lam_full = (jnp.exp(jnp.sum(lam[0:1, :] * lam[1:2, :], axis=-1, keepdims=True))
                - jnp.exp(jnp.sum(lam[2:3, :] * lam[3:4, :], axis=-1, keepdims=True)) + lambda_init)
    o = acc_ref[0] / l0 - lam_full * (acc_ref[1] / l1)
    o = o * lax.rsqrt(jnp.mean(o * o, axis=-1, keepdims=True) + LN_EPS) * sg_ref[...] * (1.0 - lambda_init)
    o_ref[...] = o.astype(o_ref.dtype)


def diff_attn(qkv, btiles, rel_bias, lam, subln_g, batch, seq, lambda_init, tq=512):
    m, d3 = qkv.shape
    d = d3 // 3
    nh = d // DIFF_V
    tq = min(tq, seq)
    tk = tq
    assert seq % tq == 0 and tk >= REL_FAR and btiles.shape == (nh, 2, tq, tk)
    nq = seq // tq
    return pl.pallas_call(
        functools.partial(_diff_attn_kernel, tq=tq, tk=tk, lambda_init=lambda_init),
        grid=(batch, nh, nq),
        in_specs=[pl.BlockSpec(memory_space=pltpu.SMEM),
                  pl.BlockSpec((tq, DIFF_V), lambda b, h, i: (b * nq + i, h)),
                  pl.BlockSpec((seq, DIFF_V), lambda b, h, i: (b, nh + h)),
                  pl.BlockSpec((seq, DIFF_V), lambda b, h, i: (b, 2 * nh + h)),
                  pl.BlockSpec((1, 2, tq, tk), lambda b, h, i: (h, 0, 0, 0)),
                  pl.BlockSpec((4, DIFF_QK), lambda b, h, i: (0, 0)),
                  pl.BlockSpec((1, DIFF_V), lambda b, h, i: (0, 0))],
        out_specs=pl.BlockSpec((tq, DIFF_V), lambda b, h, i: (b * nq + i, h)),
        out_shape=jax.ShapeDtypeStruct((m, d), BF16),
        scratch_shapes=[pltpu.VMEM((2, tq, DIFF_V), F32),
                        pltpu.VMEM((2, 2, tq, 2 * tk), F32),
                        pltpu.VMEM((2, tq, 2 * tk), BF16)],
        compiler_params=_cparams(("parallel", "parallel", "arbitrary")),
        name="diff_attn",
    )(rel_bias, qkv, qkv, qkv, btiles, lam, subln_g.reshape(1, DIFF_V))


def _shift_delta(x_ref, xh_ref, tm, seq, i):
    x = x_ref[...]
    first = (i * tm) % seq == 0
    prev = jnp.where(first, jnp.zeros((1, x.shape[1]), F32), xh_ref[SUBLANES_F32 - 1:SUBLANES_F32, :])
    row = lax.broadcasted_iota(jnp.int32, x.shape, 0)
    return x, jnp.where(row == 0, prev, pltpu.roll(x, 1, axis=0)) - x


def _rwkv_rkv_kernel(x_ref, xh_ref, mu_ref, w_ref, o_ref, *, tm, seq):
    x, xx = _shift_delta(x_ref, xh_ref, tm, seq, pl.program_id(1))
    xin = (x + xx * mu_ref[0]).astype(BF16)
    o_ref[0] = _dot(xin, w_ref[0])


def rwkv_rkv(x, mu3, wrkv, seq, tm=1024):
    m, d = x.shape
    tm = min(tm, seq)
    assert m % tm == 0 and seq % tm == 0
    r = tm // SUBLANES_F32
    return pl.pallas_call(
        functools.partial(_rwkv_rkv_kernel, tm=tm, seq=seq),
        grid=(3, m // tm),
        in_specs=[pl.BlockSpec((tm, d), lambda g, i: (i, 0)),
                  pl.BlockSpec((SUBLANES_F32, d), lambda g, i: (jnp.maximum(i * r - 1, 0), 0)),
                  pl.BlockSpec((1, 1, d), lambda g, i: (g, 0, 0)),
                  pl.BlockSpec((1, d, d), lambda g, i: (g, 0, 0))],
        out_specs=pl.BlockSpec((1, tm, d), lambda g, i: (g, i, 0)),
        out_shape=jax.ShapeDtypeStruct((3, m, d), F32),
        compiler_params=_cparams(("parallel", "parallel")),
        name="rwkv_rkv",
    )(x, x, mu3, wrkv)


def _rwkv_lora_kernel(x_ref, xh_ref, mu_ref, wl_ref, hid_ref, *, tm, seq):
    x, xx = _shift_delta(x_ref, xh_ref, tm, seq, pl.program_id(0))

    def lora(j):
        xin = (x + xx * mu_ref[j:j + 1, :]).astype(BF16)
        return _dot(xin, wl_ref[:, j * LORA_PAD:(j + 1) * LORA_PAD])

    hid_ref[:, 0 * LORA_PAD:1 * LORA_PAD] = jnp.tanh(lora(0)).astype(BF16)
    hid_ref[:, 1 * LORA_PAD:2 * LORA_PAD] = lora(1).astype(BF16)
    hid_ref[:, 2 * LORA_PAD:3 * LORA_PAD] = lora(2).astype(BF16)
    hid_ref[:, 3 * LORA_PAD:4 * LORA_PAD] = jax.nn.sigmoid(lora(3)).astype(BF16)


def rwkv_lora(x, mu4, wl, seq, tm=512):
    m, d = x.shape
    tm = min(tm, seq)
    assert m % tm == 0 and seq % tm == 0
    r = tm // SUBLANES_F32
    return pl.pallas_call(
        functools.partial(_rwkv_lora_kernel, tm=tm, seq=seq),
        grid=(m // tm,),
        in_specs=[pl.BlockSpec((tm, d), lambda i: (i, 0)),
                  pl.BlockSpec((SUBLANES_F32, d), lambda i: (jnp.maximum(i * r - 1, 0), 0)),
                  pl.BlockSpec((SUBLANES_F32, d), lambda i: (0, 0)),
                  pl.BlockSpec((d, 4 * LORA_PAD), lambda i: (0, 0))],
        out_specs=pl.BlockSpec((tm, 4 * LORA_PAD), lambda i: (i, 0)),
        out_shape=jax.ShapeDtypeStruct((m, 4 * LORA_PAD), BF16),
        compiler_params=_cparams(("parallel",)),
        name="rwkv_lora",
    )(x, x, mu4, wl)


def _softplus(x):
    return jnp.maximum(x, 0.0) + jnp.log(1.0 + jnp.exp(-jnp.abs(x)))


def _wkv_masks(tb):
    lc = WKV_CHUNK
    ri, ci = np.indices((tb, tb))
    same_chunk = (ri // lc) == (ci // lc)
    tri = [same_chunk & (ri > ci), same_chunk & (ri >= ci)]
    lev = [((ri >> (lb + 1)) == (ci >> (lb + 1))) & (((ri >> lb) & 1) == 1) & (((ci >> lb) & 1) == 0)
           for lb in range(lc.bit_length() - 1)]
    return np.stack(tri).astype(np.float32), np.stack(lev).astype(np.float32)


def _wkv_kernel(*refs, use_vres, tb, pairs):
    nin = 12 if use_vres else 10
    ins, z_ref, (s_ref, rq_ref, y1_ref, d_ref, c_ref, gam_ref, g_ref, bon_ref) = refs[:nin], refs[nin], refs[nin + 1:]
    if use_vres:
        m_ref, ml_ref, r_ref, k_ref, v_ref, vf_ref, hid_ref, w2_ref, a2_ref, v2_ref, g2_ref, vec_ref = ins
    else:
        m_ref, ml_ref, r_ref, k_ref, v_ref, hid_ref, w2_ref, a2_ref, g2_ref, vec_ref = ins
    n = RWKV_HEAD
    lc = WKV_CHUNK
    nlev = lc.bit_length() - 1
    nc = tb // lc

    @pl.when(pl.program_id(2) == 0)
    def _():
        for ref in (s_ref, rq_ref, y1_ref, d_ref, c_ref, gam_ref, g_ref, bon_ref):
            ref[...] = jnp.zeros_like(ref)

    lane = lax.broadcasted_iota(jnp.int32, (1, LANES), 1)
    head_masks = (lane < n, lane >= n)
    pr = lax.broadcasted_iota(jnp.int32, (LANES, LANES), 0)
    pc = lax.broadcasted_iota(jnp.int32, (LANES, LANES), 1)
    same_head = jnp.where((pr >= n) == (pc >= n), 1.0, 0.0)
    p_sum = same_head.astype(BF16)
    p_mean = (same_head * (1.0 / n)).astype(BF16)

    def head_reduce(x, p):
        return _dot(x.astype(BF16), p)

    hid = hid_ref[...]
    strict, incl = m_ref[0], m_ref[1]
    tril = incl.astype(BF16)
    lanes = [slice(p * LANES, (p + 1) * LANES) for p in range(pairs)]

    prev = [dict(rqb=rq_ref[p], y1=y1_ref[p], gam=gam_ref[p], g=g_ref[p], bonus=bon_ref[p], s=s_ref[p], ys=[])
            for p in range(pairs)]

    def recurrence_step(c):
        sl = slice(c * lc, (c + 1) * lc)
        for p, t in enumerate(prev):
            sb = t["s"].astype(BF16)
            t["ys"].append(_dot_nt(t["rqb"][sl], sb) + t["y1"][sl])
            t["s"] = (t["s"] + _dot(sb, d_ref[p, c]) + c_ref[p, c]) * t["gam"][c:c + 1, :]

    tok = []
    for ln in lanes:
        if len(tok) < nc:
            recurrence_step(len(tok))
        r, k, v = r_ref[0, :, ln], k_ref[0, :, ln], v_ref[0, :, ln]
        vec = vec_ref[:, ln]
        w0, a0, v0, k_k, k_a, lnx_g, lnx_b, r_k = (vec[j:j + 1, :] for j in range(8))
        wlog = -_softplus(-(w0 + _dot(hid[:, 0 * LORA_PAD:1 * LORA_PAD], w2_ref[:, ln]))) - 0.5
        ld = -jnp.exp(wlog)
        a = jax.nn.sigmoid(a0 + _dot(hid[:, 1 * LORA_PAD:2 * LORA_PAD], a2_ref[:, ln]))
        g = _dot(hid[:, 3 * LORA_PAD:4 * LORA_PAD], g2_ref[:, ln])
        if use_vres:
            v = v + (vf_ref[0, :, ln] - v) * jax.nn.sigmoid(v0 + _dot(hid[:, 2 * LORA_PAD:3 * LORA_PAD], v2_ref[:, ln]))
        kk = k * k_k
        kk = kk / jnp.maximum(jnp.sqrt(head_reduce(kk * kk, p_sum)), 1e-12)
        k2 = k * (1.0 + (a - 1.0) * k_a)
        hi, lo = _split2(ld)
        cs = _dot(tril, hi) + _dot(tril, lo)
        e_pos = jnp.exp(cs)
        e_neg = jnp.exp(-cs)
        tok.append(dict(rt=r * e_pos, at=-kk * jnp.exp(cs - ld), btb=(kk * a * e_neg).astype(BF16),
                        ktb=(k2 * e_neg).astype(BF16), vb=v.astype(BF16), e_pos=e_pos, g=g,
                        bonus=head_reduce(r * k2 * r_k, p_sum) * v))

    for c in range(pairs, nc):
        recurrence_step(c)

    for p, (ln, t) in enumerate(zip(lanes, prev)):
        s_ref[p] = t["s"]
        y = jnp.concatenate(t["ys"], axis=0)
        yc = y - head_reduce(y, p_mean)
        yn = yc * lax.rsqrt(head_reduce(yc * yc, p_mean) + GN_EPS)
        z_ref[:, ln] = ((yn * vec_ref[5:6, ln] + vec_ref[6:7, ln] + t["bonus"]) * t["g"]).astype(z_ref.dtype)

    chains = []
    for t in tok:
        rhs = jnp.concatenate([t["btb"], t["ktb"]], axis=0)
        t["y0"] = jnp.zeros((tb, LANES), F32)
        for hm in head_masks:
            at_h = jnp.where(hm, t["at"], 0.0)
            rt_h = jnp.where(hm, t["rt"], 0.0)
            amat = _dot_nt(jnp.concatenate([at_h, rt_h], axis=0).astype(BF16), rhs)
            a_k = jnp.concatenate([amat[:tb, tb:] * strict, amat[tb:, tb:] * incl], axis=0).astype(BF16)
            qy = _dot(a_k, t["vb"])
            t["y0"] = t["y0"] + jnp.where(hm, qy[tb:], 0.0)
            chains.append(dict(tok=t, hm=hm, a_rb=(amat[tb:, :tb] * incl).astype(BF16),
                               g=(amat[:tb, :tb] * strict).astype(BF16),
                               rhs=jnp.concatenate([at_h, jnp.where(hm, qy[:tb], 0.0)], axis=1)))
    for lb in range(nlev):
        for ch in chains:
            g = ch["g"]
            ch["g"] = (g.astype(F32) + _dot(g * ml_ref[lb], g)).astype(BF16)
    for t in tok:
        t["w_uv"] = jnp.zeros((tb, 2 * LANES), F32)
    for ch in chains:
        t = ch["tok"]
        t["w_uv"] = t["w_uv"] + ch["rhs"] + _dot(ch["g"], ch["rhs"].astype(BF16))
    for t in tok:
        t["w_uvb"] = t["w_uv"].astype(BF16)
        t["rq"], t["y1"] = t["rt"], t["y0"]
    for ch in chains:
        t = ch["tok"]
        x = _dot(ch["a_rb"], t["w_uvb"])
        t["rq"] = t["rq"] + jnp.where(ch["hm"], x[:, :LANES], 0.0)
        t["y1"] = t["y1"] + jnp.where(ch["hm"], x[:, LANES:], 0.0)

    trans = []
    for t in tok:
        for c in range(nc):
            sl = slice(c * lc, (c + 1) * lc)
            d_c = (_dot_tn(t["w_uvb"][sl, :LANES], t["btb"][sl]) * same_head).astype(BF16)
            c_c = _dot_tn(jnp.concatenate([t["w_uvb"][sl, LANES:], t["vb"][sl]], axis=0),
                          jnp.concatenate([t["btb"][sl], t["ktb"][sl]], axis=0)) * same_head
            trans.append((d_c, c_c))

    for p, t in enumerate(tok):
        rq_ref[p] = t["rq"].astype(BF16)
        y1_ref[p] = t["y1"]
        g_ref[p] = t["g"]
        bon_ref[p] = t["bonus"]
        for c in range(nc):
            d_ref[p, c], c_ref[p, c] = trans[p * nc + c]
            gam_ref[p, c:c + 1, :] = t["e_pos"][(c + 1) * lc - 1:(c + 1) * lc, :]


def wkv(rkv, vfirst, hid, w2, a2, v2, g2, vec, batch, seq, tb=256, pairs=4):
    _, m, d = rkv.shape
    tb = min(tb, seq)
    pairs = min(pairs, d // LANES)
    wl = pairs * LANES
    assert seq % tb == 0 and tb % WKV_CHUNK == 0 and d % wl == 0
    nt = seq // tb
    use_vres = vfirst is not None
    tri, lev = _wkv_masks(tb)
    masks, lev_masks = jnp.asarray(tri), jnp.asarray(lev, dtype=BF16)
    nc = tb // WKV_CHUNK
    blk_in = lambda b, t: b * nt + jnp.minimum(t, nt - 1)
    lora_spec = pl.BlockSpec((LORA_PAD, wl), lambda b, p, t: (0, p))
    in_specs = [pl.BlockSpec(masks.shape, lambda b, p, t: (0, 0, 0)),
                pl.BlockSpec(lev_masks.shape, lambda b, p, t: (0, 0, 0)),
                pl.BlockSpec((1, tb, wl), lambda b, p, t: (0, blk_in(b, t), p)),
                pl.BlockSpec((1, tb, wl), lambda b, p, t: (1, blk_in(b, t), p)),
                pl.BlockSpec((1, tb, wl), lambda b, p, t: (2, blk_in(b, t), p))]
    args = [masks, lev_masks, rkv, rkv, rkv]
    if use_vres:
        in_specs.append(pl.BlockSpec((1, tb, wl), lambda b, p, t: (2, blk_in(b, t), p)))
        args.append(vfirst)
    in_specs.append(pl.BlockSpec((tb, 4 * LORA_PAD), lambda b, p, t: (blk_in(b, t), 0)))
    args.append(hid)
    for wgt in ((w2, a2, v2, g2) if use_vres else (w2, a2, g2)):
        in_specs.append(lora_spec)
        args.append(wgt)
    in_specs.append(pl.BlockSpec((SUBLANES_F32, wl), lambda b, p, t: (0, p)))
    args.append(vec)
    return pl.pallas_call(
        functools.partial(_wkv_kernel, use_vres=use_vres, tb=tb, pairs=pairs),
        grid=(batch, d // wl, nt + 1),
        in_specs=in_specs,
        out_specs=pl.BlockSpec((tb, wl), lambda b, p, t: (b * nt + jnp.maximum(t - 1, 0), p)),
        out_shape=jax.ShapeDtypeStruct((m, d), BF16),
        scratch_shapes=[pltpu.VMEM((pairs, LANES, LANES), F32),
                        pltpu.VMEM((pairs, tb, LANES), BF16),
                        pltpu.VMEM((pairs, tb, LANES), F32),
                        pltpu.VMEM((pairs, nc, LANES, LANES), BF16),
                        pltpu.VMEM((pairs, nc, LANES, LANES), F32),
                        pltpu.VMEM((pairs, SUBLANES_F32, LANES), F32),
                        pltpu.VMEM((pairs, tb, LANES), F32),
                        pltpu.VMEM((pairs, tb, LANES), F32)],
        compiler_params=_cparams(("parallel", "parallel", "arbitrary")),
        name="wkv",
    )(*args)


def _pad_to(x, axis, size):
    pad = [(0, 0)] * x.ndim
    pad[axis] = (0, size - x.shape[axis])
    return jnp.pad(x, pad)


def _rwkv_layer(x, xshape, j, p, v_first):
    batch, seq, d = xshape
    mu = p["rw_mu"][j]
    mu3 = jnp.stack([mu[0], mu[2], mu[3]])[:, None, :]
    mu4 = _pad_to(jnp.stack([mu[1], mu[4], mu[3], mu[5]]), 0, SUBLANES_F32)
    use_vres = j > 0
    v1 = p["rw_v1"][j - 1] if use_vres else jnp.zeros((d, LORA_PAD), F32)
    wl = jnp.concatenate([_pad_to(w, 1, LORA_PAD) for w in (p["rw_w1"][j], p["rw_a1"][j], v1, p["rw_g1"][j])],
                         axis=1).astype(BF16)
    up = lambda w: _pad_to(w, 0, LORA_PAD).astype(BF16)
    vec = jnp.stack([p["rw_w0"][j], p["rw_a0"][j], p["rw_v0"][j - 1] if use_vres else jnp.zeros((d,), F32),
                     p["rw_k_k"][j], p["rw_k_a"][j], p["rw_lnx_g"][j], p["rw_lnx_b"][j], p["rw_r_k"][j].reshape(d)])
    rkv = rwkv_rkv(x, mu3, p["rw_w_rkv"][j].astype(BF16), seq)
    hid = rwkv_lora(x, mu4, wl, seq)
    if not use_vres:
        v_first = rkv
    z = wkv(rkv, v_first if use_vres else None, hid, up(p["rw_w2"][j]), up(p["rw_a2"][j]),
            up(p["rw_v2"][j - 1]) if use_vres else None, up(p["rw_g2"][j]), vec, batch, seq)
    return z, p["rw_w_o"][j].astype(BF16), v_first


def _attn_layer(xb, xshape, i, j, p, btiles, tq):
    batch, seq, d = xshape
    lambda_init = 0.8 - 0.6 * math.exp(-0.3 * i)
    qkv = matmul(xb, p["da_w_qkv"][j].astype(BF16), BF16, tm=2048, tn=1024)
    o = diff_attn(qkv, btiles, p["rel_bias"], p["da_lam"][j], p["da_subln_g"][j], batch, seq, lambda_init, tq=tq)
    return o, p["da_w_o"][j].astype(BF16)


def kernel(x, ln_g, ln_b, rw_mu, rw_w_rkv, rw_w0, rw_w1, rw_w2, rw_a0, rw_a1, rw_a2, rw_v0, rw_v1, rw_v2, rw_g1, rw_g2, rw_k_k, rw_k_a, rw_r_k, rw_lnx_g, rw_lnx_b, rw_w_o, da_w_qkv, da_lam, da_subln_g, da_w_o, rel_bias, ff_w_up, ff_conv_w, ff_conv_b, ff_w_down):
    p = dict(rw_mu=rw_mu, rw_w_rkv=rw_w_rkv, rw_w0=rw_w0, rw_w1=rw_w1, rw_w2=rw_w2, rw_a0=rw_a0, rw_a1=rw_a1,
             rw_a2=rw_a2, rw_v0=rw_v0, rw_v1=rw_v1, rw_v2=rw_v2, rw_g1=rw_g1, rw_g2=rw_g2, rw_k_k=rw_k_k,
             rw_k_a=rw_k_a, rw_r_k=rw_r_k, rw_lnx_g=rw_lnx_g, rw_lnx_b=rw_lnx_b, rw_w_o=rw_w_o,
             da_w_qkv=da_w_qkv, da_lam=da_lam, da_subln_g=da_subln_g, da_w_o=da_w_o, rel_bias=rel_bias)
    batch, seq, d = x.shape
    xshape = (batch, seq, d)
    f = ff_conv_b.shape[1]
    fp = -(-f // FFN_TILE) * FFN_TILE
    tq = min(ATTN_TILE, seq)
    btiles = bias_tiles(rel_bias, tq, tq)
    xf = x.reshape(batch * seq, d)
    xb = xf.astype(BF16)
    v_first = None
    for i in range(DEPTH):
        j = i // 2
        if i % 2 == 0:
            z, w_o, v_first = _rwkv_layer(xf, xshape, j, p, v_first)
        else:
            z, w_o = _attn_layer(xb, xshape, i, j, p, btiles, tq)
        xf, xb = proj_ln(z, w_o, xf, ln_g[i, 0], ln_b[i, 0])
        wu = _pad_to(ff_w_up[i][:, :f].astype(BF16), 1, fp)
        wg = _pad_to(ff_w_up[i][:, f:].astype(BF16), 1, fp)
        cw = _pad_to(_pad_to(ff_conv_w[i], 1, fp), 0, SUBLANES_F32)
        cb = _pad_to(ff_conv_b[i][None, :], 1, fp)
        h = ffn_up(xb, wu, wg, cw, cb, seq, tn=FFN_TILE)
        wd = _pad_to(ff_w_down[i].astype(BF16), 0, fp)
        xf, xb = proj_ln(h, wd, xf, ln_g[i, 1], ln_b[i, 1], tm=FFN_DOWN_ROWS)
    return xf.reshape(batch, seq, d)
```

```python
import functools
import math

import numpy as np
import jax
import jax.numpy as jnp
from jax import lax
from jax.experimental import pallas as pl
from jax.experimental.pallas import tpu as pltpu

F32 = jnp.float32
BF16 = jnp.bfloat16

DEPTH = 4
RWKV_HEAD = 64
GN_EPS = 64e-5
DIFF_QK = 128
DIFF_V = 256
REL_BUCKETS = 32
REL_MAX_EXACT = 16
REL_MAX_DIST = 128
LN_EPS = 1e-5
ALPHA = (2 * DEPTH) ** 0.25
NEG_INF = -1e30
LOG2E = math.log2(math.e)

LANES = 128
SUBLANES_F32 = 8
SUBLANES_BF16 = 16
VMEM_LIMIT_BYTES = 60000 * 1024

WKV_CHUNK = 64
LORA_PAD = 128
ATTN_TILE = 512
FFN_TILE = 512
FFN_SUBTILE = 256
FFN_SUBROWS = 512
FFN_DOWN_ROWS = 256
PROJ_SUBROWS = 256


def _cparams(sem):
    return pltpu.CompilerParams(dimension_semantics=sem, vmem_limit_bytes=VMEM_LIMIT_BYTES)


def _dot(a, b):
    return jnp.dot(a, b, preferred_element_type=F32)


def _dot_nt(a, b):
    return lax.dot_general(a, b, (((1,), (1,)), ((), ())), preferred_element_type=F32)


def _dot_tn(a, b):
    return lax.dot_general(a, b, (((0,), (0,)), ((), ())), preferred_element_type=F32)


def _split2(x):
    hi = x.astype(BF16)
    lo = (x - hi.astype(F32)).astype(BF16)
    return hi, lo


def _layer_norm(y, g, b):
    mu = jnp.mean(y, axis=-1, keepdims=True)
    yc = y - mu
    var = jnp.mean(yc * yc, axis=-1, keepdims=True)
    return yc * lax.rsqrt(var + LN_EPS) * g + b


def _mm_kernel(a_ref, w_ref, o_ref):
    o_ref[...] = _dot(a_ref[...], w_ref[...]).astype(o_ref.dtype)


def matmul(a, w, out_dtype, tm=1024, tn=1024):
    m, k = a.shape
    n = w.shape[1]
    tm, tn = min(tm, m), min(tn, n)
    assert m % tm == 0 and n % tn == 0
    return pl.pallas_call(
        _mm_kernel,
        grid=(m // tm, n // tn),
        in_specs=[pl.BlockSpec((tm, k), lambda i, j: (i, 0)),
                  pl.BlockSpec((k, tn), lambda i, j: (0, j))],
        out_specs=pl.BlockSpec((tm, tn), lambda i, j: (i, j)),
        out_shape=jax.ShapeDtypeStruct((m, n), out_dtype),
        compiler_params=_cparams(("parallel", "parallel")),
        name="matmul",
    )(a, w)


def _proj_ln_kernel(a_ref, w_ref, res_ref, g_ref, b_ref, xo_ref, xb_ref, *scratch, nk):
    tm = a_ref.shape[0]
    tr = min(PROJ_SUBROWS, tm)
    row_groups = [slice(r0, r0 + tr) for r0 in range(0, tm, tr)]

    def finish(partial):
        accs = [partial(rs) for rs in row_groups]
        for rs, acc in zip(row_groups, accs):
            o = _layer_norm(ALPHA * res_ref[rs, :] + acc, g_ref[...], b_ref[...])
            xo_ref[rs, :] = o
            xb_ref[rs, :] = o.astype(BF16)

    if nk == 1:
        finish(lambda rs: _dot(a_ref[rs, :], w_ref[...]))
        return
    acc_ref, = scratch
    k = pl.program_id(1)

    @pl.when(k == 0)
    def _():
        acc_ref[...] = _dot(a_ref[...], w_ref[...])

    @pl.when((k > 0) & (k < nk - 1))
    def _():
        acc_ref[...] += _dot(a_ref[...], w_ref[...])

    @pl.when(k == nk - 1)
    def _():
        finish(lambda rs: acc_ref[rs, :] + _dot(a_ref[rs, :], w_ref[...]))


def proj_ln(a, w, res, g, b, tm=512, tk=None):
    m, kdim = a.shape
    d = w.shape[1]
    tm = min(tm, m)
    tk = kdim if tk is None else tk
    assert m % tm == 0 and kdim % tk == 0
    nk = kdim // tk
    return pl.pallas_call(
        functools.partial(_proj_ln_kernel, nk=nk),
        grid=(m // tm, nk),
        in_specs=[pl.BlockSpec((tm, tk), lambda i, k: (i, k)),
                  pl.BlockSpec((tk, d), lambda i, k: (k, 0), **({"pipeline_mode": pl.Buffered(1)} if nk == 1 else {})),
                  pl.BlockSpec((tm, d), lambda i, k: (i, 0)),
                  pl.BlockSpec((1, d), lambda i, k: (0, 0)),
                  pl.BlockSpec((1, d), lambda i, k: (0, 0))],
        out_specs=[pl.BlockSpec((tm, d), lambda i, k: (i, 0)),
                   pl.BlockSpec((tm, d), lambda i, k: (i, 0))],
        out_shape=[jax.ShapeDtypeStruct((m, d), F32), jax.ShapeDtypeStruct((m, d), BF16)],
        scratch_shapes=[pltpu.VMEM((tm, d), F32)] if nk > 1 else [],
        compiler_params=_cparams(("parallel", "arbitrary")),
        name="proj_ln",
    )(a, w, res, g.reshape(1, d), b.reshape(1, d))


def _ffn_up_kernel(x_ref, xh_ref, wu_ref, wg_ref, cw_ref, cb_ref, h_ref, *, tm, seq):
    i = pl.program_id(0)
    x = x_ref[...]
    first = (i * tm) % seq == 0
    halo = jnp.where(first, jnp.zeros_like(xh_ref[...]), xh_ref[...])
    xe = jnp.concatenate([halo, x], axis=0)
    hl = halo.shape[0]
    tn = h_ref.shape[1]
    tr = min(FFN_SUBROWS, tm)
    subs = [(slice(r0, r0 + tr), slice(c0, min(c0 + FFN_SUBTILE, tn)))
            for r0 in range(0, tm, tr) for c0 in range(0, tn, FFN_SUBTILE)]
    ug = [(_dot(x[rs], wu_ref[:, cs]), _dot(xe[rs.start:rs.stop + hl], wg_ref[:, cs])) for rs, cs in subs]
    for (rs, cs), (u, ge) in zip(subs, ug):
        g0 = ge[hl:, :]
        g1 = pltpu.roll(ge, 1, axis=0)[hl:, :]
        g2 = pltpu.roll(ge, 2, axis=0)[hl:, :]
        gc = cb_ref[:, cs] + g2 * cw_ref[0:1, cs] + g1 * cw_ref[1:2, cs] + g0 * cw_ref[2:3, cs]
        h_ref[rs, cs] = (u * jax.nn.gelu(gc)).astype(h_ref.dtype)


def ffn_up(xb, wu, wg, conv_w, conv_b, seq, tm=4096, tn=512):
    m, d = xb.shape
    fp = wu.shape[1]
    tm, tn = min(tm, seq), min(tn, fp)
    hl = SUBLANES_BF16
    assert m % tm == 0 and fp % tn == 0 and seq % tm == 0 and tm % hl == 0
    r = tm // hl
    return pl.pallas_call(
        functools.partial(_ffn_up_kernel, tm=tm, seq=seq),
        grid=(m // tm, fp // tn),
        in_specs=[pl.BlockSpec((tm, d), lambda i, j: (i, 0)),
                  pl.BlockSpec((hl, d), lambda i, j: (jnp.maximum(i * r - 1, 0), 0)),
                  pl.BlockSpec((d, tn), lambda i, j: (0, j)),
                  pl.BlockSpec((d, tn), lambda i, j: (0, j)),
                  pl.BlockSpec((SUBLANES_F32, tn), lambda i, j: (0, j)),
                  pl.BlockSpec((1, tn), lambda i, j: (0, j))],
        out_specs=pl.BlockSpec((tm, tn), lambda i, j: (i, j)),
        out_shape=jax.ShapeDtypeStruct((m, fp), BF16),
        compiler_params=_cparams(("parallel", "parallel")),
        name="ffn_up",
    )(xb, xb, wu, wg, conv_w, conv_b)


def _bucket_thresholds():
    n = np.arange(1, 4 * REL_MAX_DIST, dtype=np.float64)
    large = REL_MAX_EXACT + np.floor(np.log(n / REL_MAX_EXACT) / math.log(REL_MAX_DIST / REL_MAX_EXACT)
                                     * (REL_BUCKETS - REL_MAX_EXACT)).astype(np.int64)
    bucket = np.where(n < REL_MAX_EXACT, n.astype(np.int64), np.minimum(large, REL_BUCKETS - 1))
    thr = [int(n[i]) for i in range(len(n)) if i == 0 or bucket[i] != bucket[i - 1]]
    assert bucket[0] == 1 and len(thr) == REL_BUCKETS - 1 and np.all(np.diff(bucket) >= 0)
    return thr


_BUCKET_THR = _bucket_thresholds()
REL_FAR = _BUCKET_THR[-1]


def _bias_tile_kernel(rb_ref, o_ref, *, tq, tk):
    h = pl.program_id(0)
    w = pl.program_id(1)
    row = lax.broadcasted_iota(jnp.int32, (tq, tk), 0)
    col = lax.broadcasted_iota(jnp.int32, (tq, tk), 1)
    dist = w * tk + row - col
    val = jnp.full((tq, tk), rb_ref[0, h], F32)
    for m, t in enumerate(_BUCKET_THR):
        val = jnp.where(dist >= t, rb_ref[m + 1, h], val)
    o_ref[0, 0] = jnp.where(dist >= 0, val * LOG2E, NEG_INF)


def bias_tiles(rel_bias, tq, tk):
    nh = rel_bias.shape[1]
    return pl.pallas_call(
        functools.partial(_bias_tile_kernel, tq=tq, tk=tk),
        grid=(nh, 2),
        in_specs=[pl.BlockSpec(memory_space=pltpu.SMEM)],
        out_specs=pl.BlockSpec((1, 1, tq, tk), lambda h, w: (h, w, 0, 0)),
        out_shape=jax.ShapeDtypeStruct((nh, 2, tq, tk), F32),
        compiler_params=_cparams(("parallel", "parallel")),
        name="bias_tiles",
    )(rel_bias)


def _diff_attn_kernel(rb_ref, q_ref, k_ref, v_ref, bias_ref, lam_ref, sg_ref, o_ref,
                      acc_ref, s_ref, p_ref, *, tq, tk, lambda_init):
    h = pl.program_id(1)
    qi = pl.program_id(2)
    q = q_ref[...].astype(F32) * (DIFF_QK ** -0.5 * LOG2E)
    qs = (q[:, :DIFF_QK].astype(BF16), q[:, DIFF_QK:].astype(BF16))
    acc_ref[...] = jnp.zeros_like(acc_ref)
    far_bias = rb_ref[REL_BUCKETS - 1, h] * LOG2E

    def scores(c, first_blk, nb, buf):
        start = pl.multiple_of(first_blk * tk, tk)
        s_ref[buf, c, :, :nb * tk] = _dot_nt(qs[c], k_ref[pl.ds(start, nb * tk), c * DIFF_QK:(c + 1) * DIFF_QK])

    def softmax(c, nb, buf, carry, near):
        w = nb * tk
        m, l = carry[2 * c], carry[2 * c + 1]
        s = s_ref[buf, c, :, :w]
        if near:
            s = s + (bias_ref[0, 0] if nb == 1 else jnp.concatenate([bias_ref[0, 1], bias_ref[0, 0]], axis=1))
            m_new = jnp.maximum(m, jnp.max(s, axis=-1, keepdims=True))
            shift = m_new
        else:
            m_new = jnp.maximum(m, jnp.max(s, axis=-1, keepdims=True) + far_bias)
            shift = m_new - far_bias
        alpha = jnp.exp2(m - m_new)
        p = jnp.exp2(s - shift)
        l_new = alpha * l + jnp.sum(p, axis=-1, keepdims=True)
        p_ref[c, :, :w] = p.astype(BF16)
        return carry[:2 * c] + (m_new, l_new) + carry[2 * c + 2:], alpha

    def values(c, first_blk, nb, alpha):
        w = nb * tk
        start = pl.multiple_of(first_blk * tk, tk)
        acc_ref[c] = alpha * acc_ref[c] + _dot(p_ref[c, :, :w], v_ref[pl.ds(start, w), :])

    def unpipelined(first_blk, nb, carry, near):
        for c in range(2):
            scores(c, first_blk, nb, 0)
        for c in range(2):
            carry, alpha = softmax(c, nb, 0, carry, near)
            values(c, first_blk, nb, alpha)
        return carry

    odd = (qi + 1) & 1
    n_pairs = (qi + 1) >> 1
    carry = (jnp.full((tq, 1), NEG_INF, F32), jnp.zeros((tq, 1), F32),
             jnp.full((tq, 1), NEG_INF, F32), jnp.zeros((tq, 1), F32))

    def lead_block(carry):
        for c in range(2):
            scores(c, 0, 1, 1)
        for c in range(2):
            scores(c, odd, 2, 0)
            carry, alpha = softmax(c, 1, 1, carry, False)
            values(c, 0, 1, alpha)
        return carry

    def first_scores(carry):
        for c in range(2):
            scores(c, odd, 2, 0)
        return carry

    def far_pair(blk, buf, carry):
        for c in range(2):
            scores(c, blk + 2, 2, 1 - buf)
            carry, alpha = softmax(c, 2, buf, carry, False)
            values(c, blk, 2, alpha)
        return carry

    def near_pair(buf, carry):
        for c in range(2):
            carry, alpha = softmax(c, 2, buf, carry, True)
            values(c, qi - 1, 2, alpha)
        return carry

    def pairs(carry):
        n_far = n_pairs - 1
        carry = lax.fori_loop(0, n_far >> 1,
                              lambda i, c: far_pair(odd + 4 * i + 2, 1, far_pair(odd + 4 * i, 0, c)), carry)
        return lax.cond((n_far & 1) == 1,
                        lambda c: near_pair(1, far_pair(odd + 2 * (n_far - 1), 0, c)),
                        lambda c: near_pair(0, c), carry)

    carry = lax.cond(qi == 0, lambda c: unpipelined(0, 1, c, True),
                     lambda c: pairs(lax.cond(odd == 1, lead_block, first_scores, c)), carry)
    l0, l1 = carry[1], carry[3]

    lam = lam_ref[...]
    lam_full = (jnp.exp(jnp.sum(lam[0:1, :] * lam[1:2, :], axis=-1, keepdims=True))
                - jnp.exp(jnp.sum(lam[2:3, :] * lam[3:4, :], axis=-1, keepdims=True)) + lambda_init)
    o = acc_ref[0] / l0 - lam_full * (acc_ref[1] / l1)
    o = o * lax.rsqrt(jnp.mean(o * o, axis=-1, keepdims=True) + LN_EPS) * sg_ref[...] * (1.0 - lambda_init)
    o_ref[...] = o.astype(o_ref.dtype)


def diff_attn(qkv, btiles, rel_bias, lam, subln_g, batch, seq, lambda_init, tq=512):
    m, d3 = qkv.shape
    d = d3 // 3
    nh = d // DIFF_V
    tq = min(tq, seq)
    tk = tq
    assert seq % tq == 0 and tk >= REL_FAR and btiles.shape == (nh, 2, tq, tk)
    nq = seq // tq
    return pl.pallas_call(
        functools.partial(_diff_attn_kernel, tq=tq, tk=tk, lambda_init=lambda_init),
        grid=(batch, nh, nq),
        in_specs=[pl.BlockSpec(memory_space=pltpu.SMEM),
                  pl.BlockSpec((tq, DIFF_V), lambda b, h, i: (b * nq + i, h)),
                  pl.BlockSpec((seq, DIFF_V), lambda b, h, i: (b, nh + h)),
                  pl.BlockSpec((seq, DIFF_V), lambda b, h, i: (b, 2 * nh + h)),
                  pl.BlockSpec((1, 2, tq, tk), lambda b, h, i: (h, 0, 0, 0)),
                  pl.BlockSpec((4, DIFF_QK), lambda b, h, i: (0, 0)),
                  pl.BlockSpec((1, DIFF_V), lambda b, h, i: (0, 0))],
        out_specs=pl.BlockSpec((tq, DIFF_V), lambda b, h, i: (b * nq + i, h)),
        out_shape=jax.ShapeDtypeStruct((m, d), BF16),
        scratch_shapes=[pltpu.VMEM((2, tq, DIFF_V), F32),
                        pltpu.VMEM((2, 2, tq, 2 * tk), F32),
                        pltpu.VMEM((2, tq, 2 * tk), BF16)],
        compiler_params=_cparams(("parallel", "parallel", "arbitrary")),
        name="diff_attn",
    )(rel_bias, qkv, qkv, qkv, btiles, lam, subln_g.reshape(1, DIFF_V))


def _shift_delta(x_ref, xh_ref, tm, seq, i):
    x = x_ref[...]
    first = (i * tm) % seq == 0
    prev = jnp.where(first, jnp.zeros((1, x.shape[1]), F32), xh_ref[SUBLANES_F32 - 1:SUBLANES_F32, :])
    row = lax.broadcasted_iota(jnp.int32, x.shape, 0)
    return x, jnp.where(row == 0, prev, pltpu.roll(x, 1, axis=0)) - x


def _rwkv_rkv_kernel(x_ref, xh_ref, mu_ref, w_ref, o_ref, *, tm, seq):
    x, xx = _shift_delta(x_ref, xh_ref, tm, seq, pl.program_id(1))
    xin = (x + xx * mu_ref[0]).astype(BF16)
    o_ref[0] = _dot(xin, w_ref[0])


def rwkv_rkv(x, mu3, wrkv, seq, tm=1024):
    m, d = x.shape
    tm = min(tm, seq)
    assert m % tm == 0 and seq % tm == 0
    r = tm // SUBLANES_F32
    return pl.pallas_call(
        functools.partial(_rwkv_rkv_kernel, tm=tm, seq=seq),
        grid=(3, m // tm),
        in_specs=[pl.BlockSpec((tm, d), lambda g, i: (i, 0)),
                  pl.BlockSpec((SUBLANES_F32, d), lambda g, i: (jnp.maximum(i * r - 1, 0), 0)),
                  pl.BlockSpec((1, 1, d), lambda g, i: (g, 0, 0)),
                  pl.BlockSpec((1, d, d), lambda g, i: (g, 0, 0))],
        out_specs=pl.BlockSpec((1, tm, d), lambda g, i: (g, i, 0)),
        out_shape=jax.ShapeDtypeStruct((3, m, d), F32),
        compiler_params=_cparams(("parallel", "parallel")),
        name="rwkv_rkv",
    )(x, x, mu3, wrkv)


def _rwkv_lora_kernel(x_ref, xh_ref, mu_ref, wl_ref, hid_ref, *, tm, seq):
    x, xx = _shift_delta(x_ref, xh_ref, tm, seq, pl.program_id(0))

    def lora(j):
        xin = (x + xx * mu_ref[j:j + 1, :]).astype(BF16)
        return _dot(xin, wl_ref[:, j * LORA_PAD:(j + 1) * LORA_PAD])

    hid_ref[:, 0 * LORA_PAD:1 * LORA_PAD] = jnp.tanh(lora(0)).astype(BF16)
    hid_ref[:, 1 * LORA_PAD:2 * LORA_PAD] = lora(1).astype(BF16)
    hid_ref[:, 2 * LORA_PAD:3 * LORA_PAD] = lora(2).astype(BF16)
    hid_ref[:, 3 * LORA_PAD:4 * LORA_PAD] = jax.nn.sigmoid(lora(3)).astype(BF16)


def rwkv_lora(x, mu4, wl, seq, tm=512):
    m, d = x.shape
    tm = min(tm, seq)
    assert m % tm == 0 and seq % tm == 0
    r = tm // SUBLANES_F32
    return pl.pallas_call(
        functools.partial(_rwkv_lora_kernel, tm=tm, seq=seq),
        grid=(m // tm,),
        in_specs=[pl.BlockSpec((tm, d), lambda i: (i, 0)),
                  pl.BlockSpec((SUBLANES_F32, d), lambda i: (jnp.maximum(i * r - 1, 0), 0)),
                  pl.BlockSpec((SUBLANES_F32, d), lambda i: (0, 0)),
                  pl.BlockSpec((d, 4 * LORA_PAD), lambda i: (0, 0))],
        out_specs=pl.BlockSpec((tm, 4 * LORA_PAD), lambda i: (i, 0)),
        out_shape=jax.ShapeDtypeStruct((m, 4 * LORA_PAD), BF16),
        compiler_params=_cparams(("parallel",)),
        name="rwkv_lora",
    )(x, x, mu4, wl)


def _softplus(x):
    return jnp.maximum(x, 0.0) + jnp.log(1.0 + jnp.exp(-jnp.abs(x)))


def _wkv_masks(tb):
    lc = WKV_CHUNK
    ri, ci = np.indices((tb, tb))
    same_chunk = (ri // lc) == (ci // lc)
    tri = [same_chunk & (ri > ci), same_chunk & (ri >= ci)]
    lev = [((ri >> (lb + 1)) == (ci >> (lb + 1))) & (((ri >> lb) & 1) == 1) & (((ci >> lb) & 1) == 0)
           for lb in range(lc.bit_length() - 1)]
    return np.stack(tri).astype(np.float32), np.stack(lev).astype(np.float32)


def _wkv_kernel(*refs, use_vres, tb, pairs):
    nin = 12 if use_vres else 10
    ins, z_ref, (s_ref, rq_ref, y1_ref, d_ref, c_ref, gam_ref, g_ref, bon_ref) = refs[:nin], refs[nin], refs[nin + 1:]
    if use_vres:
        m_ref, ml_ref, r_ref, k_ref, v_ref, vf_ref, hid_ref, w2_ref, a2_ref, v2_ref, g2_ref, vec_ref = ins
    else:
        m_ref, ml_ref, r_ref, k_ref, v_ref, hid_ref, w2_ref, a2_ref, g2_ref, vec_ref = ins
    n = RWKV_HEAD
    lc = WKV_CHUNK
    nlev = lc.bit_length() - 1
    nc = tb // lc

    @pl.when(pl.program_id(2) == 0)
    def _():
        for ref in (s_ref, rq_ref, y1_ref, d_ref, c_ref, gam_ref, g_ref, bon_ref):
            ref[...] = jnp.zeros_like(ref)

    lane = lax.broadcasted_iota(jnp.int32, (1, LANES), 1)
    head_masks = (lane < n, lane >= n)
    pr = lax.broadcasted_iota(jnp.int32, (LANES, LANES), 0)
    pc = lax.broadcasted_iota(jnp.int32, (LANES, LANES), 1)
    same_head = jnp.where((pr >= n) == (pc >= n), 1.0, 0.0)
    p_sum = same_head.astype(BF16)
    p_mean = (same_head * (1.0 / n)).astype(BF16)

    def head_reduce(x, p):
        return _dot(x.astype(BF16), p)

    hid = hid_ref[...]
    strict, incl = m_ref[0], m_ref[1]
    tril = incl.astype(BF16)
    lanes = [slice(p * LANES, (p + 1) * LANES) for p in range(pairs)]

    prev = [dict(rqb=rq_ref[p], y1=y1_ref[p], gam=gam_ref[p], g=g_ref[p], bonus=bon_ref[p], s=s_ref[p], ys=[])
            for p in range(pairs)]

    def recurrence_step(c):
        sl = slice(c * lc, (c + 1) * lc)
        for p, t in enumerate(prev):
            sb = t["s"].astype(BF16)
            t["ys"].append(_dot_nt(t["rqb"][sl], sb) + t["y1"][sl])
            t["s"] = (t["s"] + _dot(sb, d_ref[p, c]) + c_ref[p, c]) * t["gam"][c:c + 1, :]

    chains = []

    def start_chains(t):
        rhs = jnp.concatenate([t["btb"], t["ktb"]], axis=0)
        t["y0"] = jnp.zeros((tb, LANES), F32)
        for hm in head_masks:
            at_h = jnp.where(hm, t["at"], 0.0)
            rt_h = jnp.where(hm, t["rt"], 0.0)
            amat = _dot_nt(jnp.concatenate([at_h, rt_h], axis=0).astype(BF16), rhs)
            a_k = jnp.concatenate([amat[:tb, tb:] * strict, amat[tb:, tb:] * incl], axis=0).astype(BF16)
            qy = _dot(a_k, t["vb"])
            t["y0"] = t["y0"] + jnp.where(hm, qy[tb:], 0.0)
            chains.append(dict(tok=t, hm=hm, a_rb=(amat[tb:, :tb] * incl).astype(BF16),
                               g=(amat[:tb, :tb] * strict).astype(BF16),
                               rhs=jnp.concatenate([at_h, jnp.where(hm, qy[:tb], 0.0)], axis=1)))

    tok = []
    for ln in lanes:
        if len(tok) < nc:
            recurrence_step(len(tok))
        if tok:
            start_chains(tok[-1])
        r, k, v = r_ref[0, :, ln], k_ref[0, :, ln], v_ref[0, :, ln]
        vec = vec_ref[:, ln]
        w0, a0, v0, k_k, k_a, _, _, r_k = (vec[j:j + 1, :] for j in range(8))
        wlog = -_softplus(-(w0 + _dot(hid[:, 0 * LORA_PAD:1 * LORA_PAD], w2_ref[:, ln]))) - 0.5
        ld = -jnp.exp(wlog)
        a = jax.nn.sigmoid(a0 + _dot(hid[:, 1 * LORA_PAD:2 * LORA_PAD], a2_ref[:, ln]))
        g = _dot(hid[:, 3 * LORA_PAD:4 * LORA_PAD], g2_ref[:, ln])
        if use_vres:
            v = v + (vf_ref[0, :, ln] - v) * jax.nn.sigmoid(v0 + _dot(hid[:, 2 * LORA_PAD:3 * LORA_PAD], v2_ref[:, ln]))
        kk = k * k_k
        kk = kk / jnp.maximum(jnp.sqrt(head_reduce(kk * kk, p_sum)), 1e-12)
        k2 = k * (1.0 + (a - 1.0) * k_a)
        hi, lo = _split2(ld)
        cs = _dot(tril, hi) + _dot(tril, lo)
        e_pos = jnp.exp(cs)
        e_neg = jnp.exp(-cs)
        tok.append(dict(rt=r * e_pos, at=-kk * jnp.exp(cs - ld), btb=(kk * a * e_neg).astype(BF16),
                        ktb=(k2 * e_neg).astype(BF16), vb=v.astype(BF16), e_pos=e_pos, g=g,
                        bonus=head_reduce(r * k2 * r_k, p_sum) * v))

    for c in range(pairs, nc):
        recurrence_step(c)

    for p, (ln, t) in enumerate(zip(lanes, prev)):
        s_ref[p] = t["s"]
        y = jnp.concatenate(t["ys"], axis=0)
        yc = y - head_reduce(y, p_mean)
        yn = yc * lax.rsqrt(head_reduce(yc * yc, p_mean) + GN_EPS)
        lnx_g, lnx_b = vec_ref[5:6, ln], vec_ref[6:7, ln]
        z_ref[:, ln] = ((yn * lnx_g + lnx_b + t["bonus"]) * t["g"]).astype(z_ref.dtype)

    start_chains(tok[-1])
    for lb in range(nlev):
        for ch in chains:
            g = ch["g"]
            ch["g"] = (g.astype(F32) + _dot(g * ml_ref[lb], g)).astype(BF16)
    for t in tok:
        t["w_uv"] = jnp.zeros((tb, 2 * LANES), F32)
    for ch in chains:
        t = ch["tok"]
        t["w_uv"] = t["w_uv"] + ch["rhs"] + _dot(ch["g"], ch["rhs"].astype(BF16))
    for t in tok:
        t["w_uvb"] = t["w_uv"].astype(BF16)
        t["rq"], t["y1"] = t["rt"], t["y0"]
    for ch in chains:
        t = ch["tok"]
        x = _dot(ch["a_rb"], t["w_uvb"])
        t["rq"] = t["rq"] + jnp.where(ch["hm"], x[:, :LANES], 0.0)
        t["y1"] = t["y1"] + jnp.where(ch["hm"], x[:, LANES:], 0.0)

    trans = []
    for t in tok:
        for c in range(nc):
            sl = slice(c * lc, (c + 1) * lc)
            d_c = (_dot_tn(t["w_uvb"][sl, :LANES], t["btb"][sl]) * same_head).astype(BF16)
            c_c = _dot_tn(jnp.concatenate([t["w_uvb"][sl, LANES:], t["vb"][sl]], axis=0),
                          jnp.concatenate([t["btb"][sl], t["ktb"][sl]], axis=0)) * same_head
            trans.append((d_c, c_c))

    for p, t in enumerate(tok):
        rq_ref[p] = t["rq"].astype(BF16)
        y1_ref[p] = t["y1"]
        g_ref[p] = t["g"]
        bon_ref[p] = t["bonus"]
        for c in range(nc):
            d_ref[p, c], c_ref[p, c] = trans[p * nc + c]
            gam_ref[p, c:c + 1, :] = t["e_pos"][(c + 1) * lc - 1:(c + 1) * lc, :]


def wkv(rkv, vfirst, hid, w2, a2, v2, g2, vec, batch, seq, tb=256, pairs=4):
    _, m, d = rkv.shape
    tb = min(tb, seq)
    pairs = min(pairs, d // LANES)
    wl = pairs * LANES
    assert seq % tb == 0 and tb % WKV_CHUNK == 0 and d % wl == 0
    nt = seq // tb
    use_vres = vfirst is not None
    tri, lev = _wkv_masks(tb)
    masks, lev_masks = jnp.asarray(tri), jnp.asarray(lev, dtype=BF16)
    nc = tb // WKV_CHUNK
    blk_in = lambda b, t: b * nt + jnp.minimum(t, nt - 1)
    lora_spec = pl.BlockSpec((LORA_PAD, wl), lambda b, p, t: (0, p))
    in_specs = [pl.BlockSpec(masks.shape, lambda b, p, t: (0, 0, 0)),
                pl.BlockSpec(lev_masks.shape, lambda b, p, t: (0, 0, 0)),
                pl.BlockSpec((1, tb, wl), lambda b, p, t: (0, blk_in(b, t), p)),
                pl.BlockSpec((1, tb, wl), lambda b, p, t: (1, blk_in(b, t), p)),
                pl.BlockSpec((1, tb, wl), lambda b, p, t: (2, blk_in(b, t), p))]
    args = [masks, lev_masks, rkv, rkv, rkv]
    if use_vres:
        in_specs.append(pl.BlockSpec((1, tb, wl), lambda b, p, t: (2, blk_in(b, t), p)))
        args.append(vfirst)
    in_specs.append(pl.BlockSpec((tb, 4 * LORA_PAD), lambda b, p, t: (blk_in(b, t), 0)))
    args.append(hid)
    for wgt in ((w2, a2, v2, g2) if use_vres else (w2, a2, g2)):
        in_specs.append(lora_spec)
        args.append(wgt)
    in_specs.append(pl.BlockSpec((SUBLANES_F32, wl), lambda b, p, t: (0, p)))
    args.append(vec)
    return pl.pallas_call(
        functools.partial(_wkv_kernel, use_vres=use_vres, tb=tb, pairs=pairs),
        grid=(batch, d // wl, nt + 1),
        in_specs=in_specs,
        out_specs=pl.BlockSpec((tb, wl), lambda b, p, t: (b * nt + jnp.maximum(t - 1, 0), p)),
        out_shape=jax.ShapeDtypeStruct((m, d), BF16),
        scratch_shapes=[pltpu.VMEM((pairs, LANES, LANES), F32),
                        pltpu.VMEM((pairs, tb, LANES), BF16),
                        pltpu.VMEM((pairs, tb, LANES), F32),
                        pltpu.VMEM((pairs, nc, LANES, LANES), BF16),
                        pltpu.VMEM((pairs, nc, LANES, LANES), F32),
                        pltpu.VMEM((pairs, SUBLANES_F32, LANES), F32),
                        pltpu.VMEM((pairs, tb, LANES), F32),
                        pltpu.VMEM((pairs, tb, LANES), F32)],
        compiler_params=_cparams(("parallel", "parallel", "arbitrary")),
        name="wkv",
    )(*args)


def _pad_to(x, axis, size):
    pad = [(0, 0)] * x.ndim
    pad[axis] = (0, size - x.shape[axis])
    return jnp.pad(x, pad)


def _rwkv_layer(x, xshape, j, p, v_first):
    batch, seq, d = xshape
    mu = p["rw_mu"][j]
    mu3 = jnp.stack([mu[0], mu[2], mu[3]])[:, None, :]
    mu4 = _pad_to(jnp.stack([mu[1], mu[4], mu[3], mu[5]]), 0, SUBLANES_F32)
    use_vres = j > 0
    v1 = p["rw_v1"][j - 1] if use_vres else jnp.zeros((d, LORA_PAD), F32)
    wl = jnp.concatenate([_pad_to(w, 1, LORA_PAD) for w in (p["rw_w1"][j], p["rw_a1"][j], v1, p["rw_g1"][j])],
                         axis=1).astype(BF16)
    up = lambda w: _pad_to(w, 0, LORA_PAD).astype(BF16)
    vec = jnp.stack([p["rw_w0"][j], p["rw_a0"][j], p["rw_v0"][j - 1] if use_vres else jnp.zeros((d,), F32),
                     p["rw_k_k"][j], p["rw_k_a"][j], p["rw_lnx_g"][j], p["rw_lnx_b"][j], p["rw_r_k"][j].reshape(d)])
    rkv = rwkv_rkv(x, mu3, p["rw_w_rkv"][j].astype(BF16), seq)
    hid = rwkv_lora(x, mu4, wl, seq)
    if not use_vres:
        v_first = rkv
    z = wkv(rkv, v_first if use_vres else None, hid, up(p["rw_w2"][j]), up(p["rw_a2"][j]),
            up(p["rw_v2"][j - 1]) if use_vres else None, up(p["rw_g2"][j]), vec, batch, seq)
    return z, p["rw_w_o"][j].astype(BF16), v_first


def _attn_layer(xb, xshape, i, j, p, btiles, tq):
    batch, seq, d = xshape
    lambda_init = 0.8 - 0.6 * math.exp(-0.3 * i)
    qkv = matmul(xb, p["da_w_qkv"][j].astype(BF16), BF16, tm=2048, tn=1024)
    o = diff_attn(qkv, btiles, p["rel_bias"], p["da_lam"][j], p["da_subln_g"][j], batch, seq, lambda_init, tq=tq)
    return o, p["da_w_o"][j].astype(BF16)


def kernel(x, ln_g, ln_b, rw_mu, rw_w_rkv, rw_w0, rw_w1, rw_w2, rw_a0, rw_a1, rw_a2, rw_v0, rw_v1, rw_v2, rw_g1, rw_g2, rw_k_k, rw_k_a, rw_r_k, rw_lnx_g, rw_lnx_b, rw_w_o, da_w_qkv, da_lam, da_subln_g, da_w_o, rel_bias, ff_w_up, ff_conv_w, ff_conv_b, ff_w_down):
    p = dict(rw_mu=rw_mu, rw_w_rkv=rw_w_rkv, rw_w0=rw_w0, rw_w1=rw_w1, rw_w2=rw_w2, rw_a0=rw_a0, rw_a1=rw_a1,
             rw_a2=rw_a2, rw_v0=rw_v0, rw_v1=rw_v1, rw_v2=rw_v2, rw_g1=rw_g1, rw_g2=rw_g2, rw_k_k=rw_k_k,
             rw_k_a=rw_k_a, rw_r_k=rw_r_k, rw_lnx_g=rw_lnx_g, rw_lnx_b=rw_lnx_b, rw_w_o=rw_w_o,
             da_w_qkv=da_w_qkv, da_lam=da_lam, da_subln_g=da_subln_g, da_w_o=da_w_o, rel_bias=rel_bias)
    batch, seq, d = x.shape
    xshape = (batch, seq, d)
    f = ff_conv_b.shape[1]
    fp = -(-f // FFN_TILE) * FFN_TILE
    tq = min(ATTN_TILE, seq)
    btiles = bias_tiles(rel_bias, tq, tq)
    xf = x.reshape(batch * seq, d)
    xb = xf.astype(BF16)
    v_first = None
    for i in range(DEPTH):
        j = i // 2
        if i % 2 == 0:
            z, w_o, v_first = _rwkv_layer(xf, xshape, j, p, v_first)
        else:
            z, w_o = _attn_layer(xb, xshape, i, j, p, btiles, tq)
        xf, xb = proj_ln(z, w_o, xf, ln_g[i, 0], ln_b[i, 0])
        wu = _pad_to(ff_w_up[i][:, :f].astype(BF16), 1, fp)
        wg = _pad_to(ff_w_up[i][:, f:].astype(BF16), 1, fp)
        cw = _pad_to(_pad_to(ff_conv_w[i], 1, fp), 0, SUBLANES_F32)
        cb = _pad_to(ff_conv_b[i][None, :], 1, fp)
        h = ffn_up(xb, wu, wg, cw, cb, seq, tn=FFN_TILE)
        wd = _pad_to(ff_w_down[i].astype(BF16), 0, fp)
        xf, xb = proj_ln(h, wd, xf, ln_g[i, 1], ln_b[i, 1], tm=FFN_DOWN_ROWS)
    return xf.reshape(batch, seq, d)
```
